```python
import math
import jax
import jax.numpy as jnp
from jax import lax

D_MODEL = 2048
BATCH = 1
SEQ = 8192
DEPTH = 4

GRID_W = 64
CTX_LEN = 256
MIX_W = D_MODEL // 2
CHUNK = 64
GLA_HEADS = 4
GLA_DV = MIX_W // GLA_HEADS
GLA_DK = GLA_DV // 2
GLA_LR = 16
GLA_GATE_NORM = 16.0
RWKV_HEAD = 64
RWKV_HEADS = MIX_W // RWKV_HEAD
DECAY_LORA = 64
ICLR_LORA = 64
GATE_LORA = 128
RWKV_GN_EPS = 64e-5
GDN_HEAD = 128
GDN_HEADS = MIX_W // GDN_HEAD
CONV_K = 3
N_EXPERTS = 16
N_GROUPS = 4
TOP_K = 2
D_EXPERT = 1408
DISPATCH_BLOCK = 256

GLA_COLS = (GLA_HEADS * GLA_DK, GLA_HEADS * GLA_DK, MIX_W, MIX_W, GLA_LR, GLA_LR)
RWKV_COLS = (MIX_W, MIX_W, MIX_W, DECAY_LORA, DECAY_LORA, ICLR_LORA, ICLR_LORA, GATE_LORA)
GDN_COLS = (3 * MIX_W, MIX_W, GDN_HEADS, GDN_HEADS, GDN_HEADS, GDN_HEADS)
GLA_PROJ = sum(GLA_COLS)
RWKV_PROJ = sum(RWKV_COLS)
GDN_PROJ = sum(GDN_COLS)
IN_COLS = (GLA_PROJ, RWKV_PROJ, GDN_PROJ, 3 * D_MODEL)
D_IN = sum(IN_COLS)

kernel_name = 'hybrid_gla_rwkv7_gdn_moe_flow_block'


def _split_cols(p, widths):
    out, s = [], 0
    for w in widths:
        out.append(p[..., s:s + w])
        s += w
    return out


def _layer_norm(x, g, b, eps=1e-5):
    xf = x.astype(jnp.float32)
    mu = xf.mean(-1, keepdims=True)
    var = jnp.square(xf - mu).mean(-1, keepdims=True)
    return ((xf - mu) * lax.rsqrt(var + eps)).astype(x.dtype) * g + b


def _rms_norm(x, w, eps=1e-6):
    xf = x.astype(jnp.float32)
    return (xf * lax.rsqrt(jnp.square(xf).mean(-1, keepdims=True) + eps)).astype(x.dtype) * w


def _l2_normalize(x, eps=1e-6):
    xf = x.astype(jnp.float32)
    return (xf * lax.rsqrt(jnp.square(xf).sum(-1, keepdims=True) + eps)).astype(x.dtype)


def _to_chunks(t):
    b, n = t.shape[:2]
    t = t.reshape((b, n // CHUNK, CHUNK) + t.shape[2:])
    return jnp.swapaxes(t, 2, 3)


def _from_chunks(t):
    t = jnp.swapaxes(t, 2, 3)
    return t.reshape((t.shape[0], t.shape[1] * t.shape[2]) + t.shape[3:])


def _token_shift(p):
    pad = jnp.pad(p, ((0, 0), (1, 1), (0, 0)))
    return 0.5 * (pad[:, :-2] + pad[:, 2:])


def _grid_conv(t, w):
    b, n, ch = t.shape
    rows = n // GRID_W
    y = lax.conv_general_dilated(t.reshape(b, rows, GRID_W, ch), w[:, :, None, :], (1, 1), 'SAME',
                                 dimension_numbers=('NHWC', 'HWIO', 'NHWC'), feature_group_count=ch)
    return y.reshape(b, n, ch)


def _seq_conv(t, w):
    return lax.conv_general_dilated(t, w[:, None, :], (1,), 'SAME',
                                    dimension_numbers=('NWC', 'WIO', 'NWC'), feature_group_count=t.shape[-1])


def _two_stage(scan_fn, ctx_args, lat_args, state0, reverse):
    if reverse:
        ctx_args = tuple(jnp.flip(t, axis=1) for t in ctx_args)
        lat_args = tuple(jnp.flip(t, axis=1) for t in lat_args)
    o_ctx, s_ctx = scan_fn(*ctx_args, state0)
    o_lat, _ = scan_fn(*lat_args, s_ctx)
    if reverse:
        o_ctx, o_lat = jnp.flip(o_ctx, axis=1), jnp.flip(o_lat, axis=1)
    return o_ctx, o_lat


def _gla_chunked(q, k, v, g, s0):
    dt = v.dtype
    q, k, v, g = (_to_chunks(t) for t in (q, k, v, g))
    b = jnp.cumsum(g.astype(jnp.float32), axis=3)
    b_ref = b[:, :, :, CHUNK // 2 - 1:CHUNK // 2]
    b_last = b[:, :, :, -1:]
    lower = jnp.tril(jnp.ones((CHUNK, CHUNK), bool))
    q_rel = q * jnp.exp(b - b_ref).astype(dt)
    k_rel = k * jnp.exp(b_ref - b).astype(dt)
    att = jnp.where(lower, jnp.einsum('bnhid,bnhjd->bnhij', q_rel, k_rel), 0)
    o_intra = jnp.einsum('bnhij,bnhjv->bnhiv', att, v)
    k_end = k * jnp.exp(b_last - b).astype(dt)
    ds = jnp.einsum('bnhjd,bnhjv->bnhdv', k_end, v)
    dec = jnp.exp(b_last[:, :, :, 0]).astype(dt)

    def step(s, inp):
        dec_n, ds_n = inp
        return dec_n[..., None] * s + ds_n, s

    s_fin, s_in = lax.scan(step, s0, (jnp.moveaxis(dec, 1, 0), jnp.moveaxis(ds, 1, 0)))
    o = o_intra + jnp.einsum('bnhid,bnhdv->bnhiv', q * jnp.exp(b).astype(dt), jnp.moveaxis(s_in, 0, 1))
    return _from_chunks(o), s_fin


def _gdn_chunked(q, k, v, log_a, beta, s0):
    dt = v.dtype
    f32 = jnp.float32
    q, k, v, log_a, beta = (_to_chunks(t.astype(f32)) for t in (q, k, v, log_a, beta))
    gam = jnp.cumsum(log_a, axis=-1)
    incl = jnp.tril(jnp.ones((CHUNK, CHUNK), bool))
    strict = jnp.tril(jnp.ones((CHUNK, CHUNK), bool), -1)
    decay = jnp.exp(jnp.where(incl, gam[..., :, None] - gam[..., None, :], -jnp.inf))
    a_mat = jnp.where(strict, beta[..., :, None] * jnp.einsum('bnhid,bnhjd->bnhij', k, k) * decay, 0.0)
    rhs = jnp.concatenate([v * beta[..., None], k * (beta * jnp.exp(gam))[..., None]], axis=-1)
    sol = lax.linalg.triangular_solve(a_mat, rhs, left_side=True, lower=True, unit_diagonal=True)
    dv = v.shape[-1]
    u, w = sol[..., :dv], sol[..., dv:]
    qk = jnp.einsum('bnhid,bnhjd->bnhij', q, k) * decay
    q_dec = q * jnp.exp(gam)[..., None]
    k_end = k * jnp.exp(gam[..., -1:] - gam)[..., None]
    dec = jnp.exp(gam[..., -1])

    def step(s, inp):
        u_n, w_n, qk_n, q_n, k_n, dec_n = inp
        delta = u_n - jnp.einsum('bhcd,bhdv->bhcv', w_n, s)
        o_n = jnp.einsum('bhcd,bhdv->bhcv', q_n, s) + jnp.einsum('bhij,bhjv->bhiv', qk_n, delta)
        s = dec_n[..., None, None] * s + jnp.einsum('bhcd,bhcv->bhdv', k_n, delta)
        return s, o_n

    xs = tuple(jnp.moveaxis(t, 1, 0) for t in (u, w, qk, q_dec, k_end, dec))
    s_fin, o = lax.scan(step, s0, xs)
    return _from_chunks(jnp.moveaxis(o, 0, 1)).astype(dt), s_fin


def _rwkv7_scan(r, w, k, v, kk, b, s0):
    def step(s, inp):
        r_t, w_t, k_t, v_t, kk_t, b_t = inp
        sa = jnp.einsum('bhvk,bhk->bhv', s, kk_t)
        s = s * w_t[:, :, None, :] - sa[..., None] * b_t[:, :, None, :] + v_t[..., None] * k_t[:, :, None, :]
        return s, jnp.einsum('bhvk,bhk->bhv', s, r_t)

    s_fin, y = lax.scan(step, s0, tuple(jnp.moveaxis(t, 1, 0) for t in (r, w, k, v, kk, b)))
    return jnp.moveaxis(y, 0, 1), s_fin


def _gla_mixer(pc, pl, dec_w2, dec_b, norm_w):
    def prep(p):
        bsz, n = p.shape[:2]
        q, k, v, og, lr_f, lr_b = _split_cols(p, GLA_COLS)
        heads = lambda t, d: t.reshape(bsz, n, GLA_HEADS, d)
        g = [heads(jax.nn.log_sigmoid((lr @ dec_w2[d] + dec_b[d]).astype(jnp.float32)) / GLA_GATE_NORM, GLA_DK)
             for d, lr in enumerate((lr_f, lr_b))]
        return heads(q, GLA_DK) * GLA_DK ** -0.5, heads(k, GLA_DK), heads(v, GLA_DV), og, g

    qc, kc, vc, ogc, gc = prep(pc)
    ql, kl, vl, ogl, gl = prep(pl)
    s0 = jnp.zeros((pc.shape[0], GLA_HEADS, GLA_DK, GLA_DV), pc.dtype)
    cf, lf = _two_stage(_gla_chunked, (qc, kc, vc, gc[0]), (ql, kl, vl, gl[0]), s0, False)
    cb, lb = _two_stage(_gla_chunked, (qc, kc, vc, gc[1]), (ql, kl, vl, gl[1]), s0, True)

    def finish(o, og):
        return _rms_norm(o, norm_w).reshape(og.shape) * jax.nn.silu(og)

    return finish(cf + cb, ogc), finish(lf + lb, ogl)


def _rwkv_mixer(pc, pl, mu, w2, w0, a2, a0, g2, k_k, k_a, r_k, ln_w, ln_b):
    def prep(p):
        bsz, n = p.shape[:2]
        heads = lambda t: t.reshape(bsz, n, RWKV_HEADS, RWKV_HEAD)
        p = p + (_token_shift(p) - p) * mu
        r, k, v, wl_f, wl_b, al_f, al_b, gl = _split_cols(p, RWKV_COLS)
        kk = _l2_normalize(heads(k * k_k))
        per_dir = []
        for d, (wl, al) in enumerate(((wl_f, al_f), (wl_b, al_b))):
            z = w0[d] + jnp.tanh(wl) @ w2[d]
            decay = jnp.exp(-jnp.exp(-jax.nn.softplus(-z) - 0.5))
            a = jax.nn.sigmoid(a0[d] + al @ a2[d])
            per_dir.append((heads(decay), heads(k * (1 + (a - 1) * k_a)), kk * heads(a)))
        g = jax.nn.sigmoid(gl) @ g2
        return heads(r), heads(k), heads(v), kk, per_dir, g

    rc, kc, vc, kkc, dc, gc = prep(pc)
    rl, kl, vl, kkl, dl, gl = prep(pl)
    s0 = jnp.zeros((pc.shape[0], RWKV_HEADS, RWKV_HEAD, RWKV_HEAD), pc.dtype)
    cf, lf = _two_stage(_rwkv7_scan, (rc, dc[0][0], dc[0][1], vc, kkc, dc[0][2]),
                        (rl, dl[0][0], dl[0][1], vl, kkl, dl[0][2]), s0, False)
    cb, lb = _two_stage(_rwkv7_scan, (rc, dc[1][0], dc[1][1], vc, kkc, dc[1][2]),
                        (rl, dl[1][0], dl[1][1], vl, kkl, dl[1][2]), s0, True)

    def finish(y, r, k, v, g):
        yf = y.astype(jnp.float32)
        m = yf.mean(-1, keepdims=True)
        var = jnp.square(yf - m).mean(-1, keepdims=True)
        y = ((yf - m) * lax.rsqrt(var + RWKV_GN_EPS)).astype(y.dtype)
        y = y * ln_w.reshape(RWKV_HEADS, RWKV_HEAD) + ln_b.reshape(RWKV_HEADS, RWKV_HEAD)
        y = y + jnp.sum(r * k * r_k, axis=-1, keepdims=True) * v
        return y.reshape(g.shape) * g

    return finish(cf + cb, rc, kc, vc, gc), finish(lf + lb, rl, kl, vl, gl)


def _gdn_mixer(pc, pl, conv_w, a_log, dt_bias, norm_w):
    def prep(p, on_grid):
        bsz, n = p.shape[:2]
        heads = lambda t: t.reshape(bsz, n, GDN_HEADS, GDN_HEAD)
        qkv, z, a_f, a_b, b_f, b_b = _split_cols(p, GDN_COLS)
        qkv = jax.nn.silu(_grid_conv(qkv, conv_w) if on_grid else _seq_conv(qkv, conv_w[CONV_K // 2]))
        q, k, v = jnp.split(qkv, 3, axis=-1)
        per_dir = [(-jnp.exp(a_log[d]) * jax.nn.softplus(a + dt_bias[d]), jax.nn.sigmoid(b))
                   for d, (a, b) in enumerate(((a_f, b_f), (a_b, b_b)))]
        return _l2_normalize(heads(q)) * GDN_HEAD ** -0.5, _l2_normalize(heads(k)), heads(v), z, per_dir

    qc, kc, vc, zc, dc = prep(pc, False)
    ql, kl, vl, zl, dl = prep(pl, True)
    s0 = jnp.zeros((pc.shape[0], GDN_HEADS, GDN_HEAD, GDN_HEAD), jnp.float32)
    cf, lf = _two_stage(_gdn_chunked, (qc, kc, vc) + dc[0], (ql, kl, vl) + dl[0], s0, False)
    cb, lb = _two_stage(_gdn_chunked, (qc, kc, vc) + dc[1], (ql, kl, vl) + dl[1], s0, True)

    def finish(o, z):
        return _rms_norm(o, norm_w).reshape(z.shape) * jax.nn.silu(z)

    return finish(cf + cb, zc), finish(lf + lb, zl)


def _merge(gate_pre, outs, w_branch, w_out):
    gates = jax.nn.sigmoid(gate_pre)
    y = 0
    for m, o in enumerate(outs):
        y = y + gates[..., m * D_MODEL:(m + 1) * D_MODEL] * (o @ w_branch[m])
    return y @ w_out


def _moe_ffn(h, router_w, router_bias, w1, w3, w2):
    n_tok, d = h.shape
    scores = jax.nn.sigmoid((h @ router_w).astype(jnp.float32))
    sel = scores + router_bias.astype(jnp.float32)
    grp_top = lax.top_k(sel.reshape(n_tok, N_GROUPS, N_EXPERTS // N_GROUPS), 2)[0]
    best_group = jnp.argmax(grp_top.sum(-1), axis=-1)
    in_group = (jnp.arange(N_EXPERTS) // (N_EXPERTS // N_GROUPS))[None, :] == best_group[:, None]
    _, top_e = lax.top_k(jnp.where(in_group, sel, -jnp.inf), TOP_K)
    top_s = jnp.take_along_axis(scores, top_e, axis=-1)
    top_w = (top_s / top_s.sum(-1, keepdims=True)).astype(h.dtype)
    n_asg = n_tok * TOP_K
    flat_e = top_e.reshape(-1)
    order = jnp.argsort(flat_e)
    sorted_e = flat_e[order]
    counts = jnp.bincount(flat_e, length=N_EXPERTS)
    padded = (counts + DISPATCH_BLOCK - 1) // DISPATCH_BLOCK * DISPATCH_BLOCK
    pad_end = jnp.cumsum(padded)
    first = jnp.cumsum(counts) - counts
    dest = (pad_end - padded)[sorted_e] + jnp.arange(n_asg) - first[sorted_e]
    n_blk = -(-n_asg // DISPATCH_BLOCK) + N_EXPERTS
    tok = jnp.repeat(jnp.arange(n_tok, dtype=jnp.int32), TOP_K)
    slot_tok = jnp.full((n_blk * DISPATCH_BLOCK,), n_tok, jnp.int32).at[dest].set(tok[order])
    slot_w = jnp.zeros((n_blk * DISPATCH_BLOCK,), h.dtype).at[dest].set(top_w.reshape(-1)[order])
    blk_e = jnp.minimum(jnp.searchsorted(pad_end, jnp.arange(n_blk) * DISPATCH_BLOCK, side='right'), N_EXPERTS - 1)
    h_pad = jnp.concatenate([h, jnp.zeros((1, d), h.dtype)], axis=0)
    xb = h_pad[slot_tok].reshape(n_blk, DISPATCH_BLOCK, d)

    def expert_block(args):
        xe, e = args
        return (jax.nn.silu(xe @ w1[e]) * (xe @ w3[e])) @ w2[e]

    yb = lax.map(expert_block, (xb, blk_e))
    y = jnp.zeros((n_tok + 1, d), h.dtype).at[slot_tok].add(yb.reshape(-1, d) * slot_w[:, None])
    return y[:n_tok]


def setup_inputs(seed: int = 0) -> dict:
    key = jax.random.key(seed)
    keys = iter(jax.random.split(key, 48))

    def nrm(shape, scale):
        return jax.random.normal(next(keys), shape, jnp.float32) * scale

    def unif(shape, lo, hi):
        return jax.random.uniform(next(keys), shape, jnp.float32, lo, hi)

    D, L = D_MODEL, DEPTH
    beta = (8.0 * DEPTH) ** -0.25
    dt = jnp.exp(unif((L, 2, GDN_HEADS), math.log(1e-3), math.log(1e-1)))
    return {
        'x': nrm((BATCH, SEQ, D), 1.0),
        'c': nrm((BATCH, D), 1.0),
        'ctx': nrm((BATCH, CTX_LEN, D), 1.0),
        'c_ctx': nrm((D,), 1.0),
        'w_ada': nrm((L, D, 6 * D), 0.5 * D ** -0.5),
        'b_ada': nrm((L, 6 * D), 0.01),
        'w_in': nrm((L, D, D_IN), D ** -0.5),
        'b_in': nrm((L, D_IN), 0.01),
        'gla_dec_w': nrm((L, 2, GLA_LR, GLA_HEADS * GLA_DK), GLA_LR ** -0.5),
        'gla_dec_b': 1.0 + nrm((L, 2, GLA_HEADS * GLA_DK), 0.1),
        'gla_norm_w': 1.0 + nrm((L, GLA_DV), 0.02),
        'rwkv_mu': unif((L, RWKV_PROJ), 0.2, 0.8),
        'rwkv_w2': nrm((L, 2, DECAY_LORA, MIX_W), 0.1 * DECAY_LORA ** -0.5),
        'rwkv_w0': -2.0 + nrm((L, 2, MIX_W), 0.5),
        'rwkv_a2': nrm((L, 2, ICLR_LORA, MIX_W), 0.1 * ICLR_LORA ** -0.5),
        'rwkv_a0': nrm((L, 2, MIX_W), 0.1),
        'rwkv_g2': nrm((L, GATE_LORA, MIX_W), GATE_LORA ** -0.5),
        'rwkv_kk': 0.85 + nrm((L, MIX_W), 0.02),
        'rwkv_ka': 1.0 + nrm((L, MIX_W), 0.02),
        'rwkv_rk': nrm((L, RWKV_HEADS, RWKV_HEAD), 0.1),
        'rwkv_ln_w': 1.0 + nrm((L, MIX_W), 0.02),
        'rwkv_ln_b': nrm((L, MIX_W), 0.01),
        'gdn_conv_w': nrm((L, CONV_K, CONV_K, 3 * MIX_W), 1.0 / CONV_K),
        'gdn_a_log': jnp.log(unif((L, 2, GDN_HEADS), 1.0, 16.0)),
        'gdn_dt_bias': dt + jnp.log(-jnp.expm1(-dt)),
        'gdn_norm_w': 1.0 + nrm((L, GDN_HEAD), 0.02),
        'w_branch': nrm((L, 3, MIX_W, D), MIX_W ** -0.5),
        'w_out': nrm((L, D, D), beta * D ** -0.5),
        'ln1_g': 1.0 + nrm((L, D), 0.02),
        'ln1_b': nrm((L, D), 0.01),
        'router_w': nrm((D, N_EXPERTS), D ** -0.5),
        'router_bias': nrm((N_EXPERTS,), 0.01),
        'moe_w1': nrm((L, N_EXPERTS, D, D_EXPERT), D ** -0.5),
        'moe_w3': nrm((L, N_EXPERTS, D, D_EXPERT), D ** -0.5),
        'moe_w2': nrm((L, N_EXPERTS, D_EXPERT, D), beta * D_EXPERT ** -0.5),
        'ln2_g': 1.0 + nrm((L, D), 0.02),
        'ln2_b': nrm((L, D), 0.01),
    }


def reference(x, c, ctx, c_ctx, w_ada, b_ada, w_in, b_in,
              gla_dec_w, gla_dec_b, gla_norm_w,
              rwkv_mu, rwkv_w2, rwkv_w0, rwkv_a2, rwkv_a0, rwkv_g2, rwkv_kk, rwkv_ka, rwkv_rk,
              rwkv_ln_w, rwkv_ln_b,
              gdn_conv_w, gdn_a_log, gdn_dt_bias, gdn_norm_w,
              w_branch, w_out, ln1_g, ln1_b,
              router_w, router_bias, moe_w1, moe_w3, moe_w2, ln2_g, ln2_b):
    alpha = (2.0 * DEPTH) ** 0.25
    bsz, n_lat, d = x.shape
    n_ctx = ctx.shape[1]
    silu_c = jax.nn.silu(c)[:, None, :]
    silu_cc = jax.nn.silu(c_ctx)
    xl, xc = x, ctx
    for l in range(DEPTH):
        last = l == DEPTH - 1
        mod_l = jnp.split(silu_c @ w_ada[l] + b_ada[l], 6, axis=-1)
        mod_c = jnp.split(silu_cc @ w_ada[l] + b_ada[l], 6, axis=-1)
        hl = xl * (1 + mod_l[1]) + mod_l[0]
        hc = xc * (1 + mod_c[1]) + mod_c[0]
        pl = hl @ w_in[l] + b_in[l]
        pc = hc @ w_in[l] + b_in[l]
        gla_c, rwkv_c, gdn_c, gate_c = _split_cols(pc, IN_COLS)
        gla_l, rwkv_l, gdn_l, gate_l = _split_cols(pl, IN_COLS)
        oa_c, oa_l = _gla_mixer(gla_c, gla_l, gla_dec_w[l], gla_dec_b[l], gla_norm_w[l])
        ob_c, ob_l = _rwkv_mixer(rwkv_c, rwkv_l, rwkv_mu[l], rwkv_w2[l], rwkv_w0[l], rwkv_a2[l], rwkv_a0[l],
                                 rwkv_g2[l], rwkv_kk[l], rwkv_ka[l], rwkv_rk[l], rwkv_ln_w[l], rwkv_ln_b[l])
        oc_c, oc_l = _gdn_mixer(gdn_c, gdn_l, gdn_conv_w[l], gdn_a_log[l], gdn_dt_bias[l], gdn_norm_w[l])
        mix_l = _merge(gate_l, (oa_l, ob_l, oc_l), w_branch[l], w_out[l])
        xl = _layer_norm(alpha * xl + mod_l[2] * mix_l, ln1_g[l], ln1_b[l])
        h2l = xl * (1 + mod_l[4]) + mod_l[3]
        if last:
            yl = _moe_ffn(h2l.reshape(-1, d), router_w, router_bias,
                          moe_w1[l], moe_w3[l], moe_w2[l]).reshape(xl.shape)
        else:
            mix_c = _merge(gate_c, (oa_c, ob_c, oc_c), w_branch[l], w_out[l])
            xc = _layer_norm(alpha * xc + mod_c[2] * mix_c, ln1_g[l], ln1_b[l])
            h2c = xc * (1 + mod_c[4]) + mod_c[3]
            y = _moe_ffn(jnp.concatenate([h2c, h2l], axis=1).reshape(-1, d), router_w, router_bias,
                         moe_w1[l], moe_w3[l], moe_w2[l]).reshape(bsz, n_ctx + n_lat, d)
            xc = _layer_norm(alpha * xc + mod_c[5] * y[:, :n_ctx], ln2_g[l], ln2_b[l])
            yl = y[:, n_ctx:]
        xl = _layer_norm(alpha * xl + mod_l[5] * yl, ln2_g[l], ln2_b[l])
    return xl
```

```python
import functools
import math

import jax
import jax.numpy as jnp
from jax import lax
from jax.experimental import pallas as pl
from jax.experimental.pallas import tpu as pltpu

D_MODEL = 2048
DEPTH = 4
GRID_W = 64
MIX_W = D_MODEL // 2
CHUNK = 64
GLA_HEADS = 4
GLA_DV = MIX_W // GLA_HEADS
GLA_DK = GLA_DV // 2
GLA_LR = 16
GLA_GATE_NORM = 16.0
RWKV_HEAD = 64
RWKV_HEADS = MIX_W // RWKV_HEAD
DECAY_LORA = 64
ICLR_LORA = 64
GATE_LORA = 128
RWKV_GN_EPS = 64e-5
GDN_HEAD = 128
GDN_HEADS = MIX_W // GDN_HEAD
CONV_K = 3
N_EXPERTS = 16
N_GROUPS = 4
TOP_K = 2
D_EXPERT = 1408
DISPATCH_BLOCK = 256

GLA_COLS = (GLA_HEADS * GLA_DK, GLA_HEADS * GLA_DK, MIX_W, MIX_W, GLA_LR, GLA_LR)
RWKV_COLS = (MIX_W, MIX_W, MIX_W, DECAY_LORA, DECAY_LORA, ICLR_LORA, ICLR_LORA, GATE_LORA)
GDN_COLS = (3 * MIX_W, MIX_W, GDN_HEADS, GDN_HEADS, GDN_HEADS, GDN_HEADS)
GLA_PROJ = sum(GLA_COLS)
RWKV_PROJ = sum(RWKV_COLS)
GDN_PROJ = sum(GDN_COLS)
IN_COLS = (GLA_PROJ, RWKV_PROJ, GDN_PROJ, 3 * D_MODEL)

VMEM_LIMIT = 48 * 1024 * 1024
HIGHEST = lax.Precision.HIGHEST


def _split_cols(p, widths):
    out, s = [], 0
    for w in widths:
        out.append(p[..., s:s + w])
        s += w
    return out


def _layer_norm(x, g, b, eps=1e-5):
    xf = x.astype(jnp.float32)
    mu = xf.mean(-1, keepdims=True)
    var = jnp.square(xf - mu).mean(-1, keepdims=True)
    return ((xf - mu) * lax.rsqrt(var + eps)).astype(x.dtype) * g + b


def _rms_norm(x, w, eps=1e-6):
    xf = x.astype(jnp.float32)
    return (xf * lax.rsqrt(jnp.square(xf).mean(-1, keepdims=True) + eps)).astype(x.dtype) * w


def _l2_normalize(x, eps=1e-6):
    xf = x.astype(jnp.float32)
    return (xf * lax.rsqrt(jnp.square(xf).sum(-1, keepdims=True) + eps)).astype(x.dtype)


def _to_chunks(t):
    b, n = t.shape[:2]
    t = t.reshape((b, n // CHUNK, CHUNK) + t.shape[2:])
    return jnp.swapaxes(t, 2, 3)


def _from_chunks(t):
    t = jnp.swapaxes(t, 2, 3)
    return t.reshape((t.shape[0], t.shape[1] * t.shape[2]) + t.shape[3:])


def _token_shift(p):
    pad = jnp.pad(p, ((0, 0), (1, 1), (0, 0)))
    return 0.5 * (pad[:, :-2] + pad[:, 2:])


def _grid_conv(t, w):
    b, n, ch = t.shape
    rows = n // GRID_W
    y = lax.conv_general_dilated(t.reshape(b, rows, GRID_W, ch), w[:, :, None, :], (1, 1), 'SAME',
                                 dimension_numbers=('NHWC', 'HWIO', 'NHWC'), feature_group_count=ch)
    return y.reshape(b, n, ch)


def _seq_conv(t, w):
    return lax.conv_general_dilated(t, w[:, None, :], (1,), 'SAME',
                                    dimension_numbers=('NWC', 'WIO', 'NWC'), feature_group_count=t.shape[-1])


def _two_stage(scan_fn, ctx_args, lat_args, state0, reverse):
    if reverse:
        ctx_args = tuple(jnp.flip(t, axis=1) for t in ctx_args)
        lat_args = tuple(jnp.flip(t, axis=1) for t in lat_args)
    o_ctx, s_ctx = scan_fn(*ctx_args, state0)
    o_lat, _ = scan_fn(*lat_args, s_ctx)
    if reverse:
        o_ctx, o_lat = jnp.flip(o_ctx, axis=1), jnp.flip(o_lat, axis=1)
    return o_ctx, o_lat


def _gla_chunked(q, k, v, g, s0):
    dt = v.dtype
    q, k, v, g = (_to_chunks(t) for t in (q, k, v, g))
    b = jnp.cumsum(g.astype(jnp.float32), axis=3)
    b_ref = b[:, :, :, CHUNK // 2 - 1:CHUNK // 2]
    b_last = b[:, :, :, -1:]
    lower = jnp.tril(jnp.ones((CHUNK, CHUNK), bool))
    q_rel = q * jnp.exp(b - b_ref).astype(dt)
    k_rel = k * jnp.exp(b_ref - b).astype(dt)
    att = jnp.where(lower, jnp.einsum('bnhid,bnhjd->bnhij', q_rel, k_rel), 0)
    o_intra = jnp.einsum('bnhij,bnhjv->bnhiv', att, v)
    k_end = k * jnp.exp(b_last - b).astype(dt)
    ds = jnp.einsum('bnhjd,bnhjv->bnhdv', k_end, v)
    dec = jnp.exp(b_last[:, :, :, 0]).astype(dt)

    def step(s, inp):
        dec_n, ds_n = inp
        return dec_n[..., None] * s + ds_n, s

    s_fin, s_in = lax.scan(step, s0, (jnp.moveaxis(dec, 1, 0), jnp.moveaxis(ds, 1, 0)))
    o = o_intra + jnp.einsum('bnhid,bnhdv->bnhiv', q * jnp.exp(b).astype(dt), jnp.moveaxis(s_in, 0, 1))
    return _from_chunks(o), s_fin


def _gdn_chunked(q, k, v, log_a, beta, s0):
    dt = v.dtype
    f32 = jnp.float32
    q, k, v, log_a, beta = (_to_chunks(t.astype(f32)) for t in (q, k, v, log_a, beta))
    gam = jnp.cumsum(log_a, axis=-1)
    incl = jnp.tril(jnp.ones((CHUNK, CHUNK), bool))
    strict = jnp.tril(jnp.ones((CHUNK, CHUNK), bool), -1)
    decay = jnp.exp(jnp.where(incl, gam[..., :, None] - gam[..., None, :], -jnp.inf))
    a_mat = jnp.where(strict, beta[..., :, None] * jnp.einsum('bnhid,bnhjd->bnhij', k, k) * decay, 0.0)
    rhs = jnp.concatenate([v * beta[..., None], k * (beta * jnp.exp(gam))[..., None]], axis=-1)
    sol = lax.linalg.triangular_solve(a_mat, rhs, left_side=True, lower=True, unit_diagonal=True)
    dv = v.shape[-1]
    u, w = sol[..., :dv], sol[..., dv:]
    qk = jnp.einsum('bnhid,bnhjd->bnhij', q, k) * decay
    q_dec = q * jnp.exp(gam)[..., None]
    k_end = k * jnp.exp(gam[..., -1:] - gam)[..., None]
    dec = jnp.exp(gam[..., -1])

    def step(s, inp):
        u_n, w_n, qk_n, q_n, k_n, dec_n = inp
        delta = u_n - jnp.einsum('bhcd,bhdv->bhcv', w_n, s)
        o_n = jnp.einsum('bhcd,bhdv->bhcv', q_n, s) + jnp.einsum('bhij,bhjv->bhiv', qk_n, delta)
        s = dec_n[..., None, None] * s + jnp.einsum('bhcd,bhcv->bhdv', k_n, delta)
        return s, o_n

    xs = tuple(jnp.moveaxis(t, 1, 0) for t in (u, w, qk, q_dec, k_end, dec))
    s_fin, o = lax.scan(step, s0, xs)
    return _from_chunks(jnp.moveaxis(o, 0, 1)).astype(dt), s_fin


RWKV_HEAD_GROUP = 8


def _bmm(a, b):
    return jnp.einsum('hmk,hkn->hmn', a, b, preferred_element_type=jnp.float32)


def _bmm_nt(a, b):
    return jnp.einsum('hmk,hnk->hmn', a, b, preferred_element_type=jnp.float32)


def _bmm_tn(a, b):
    return jnp.einsum('hkm,hkn->hmn', a, b, preferred_element_type=jnp.float32)


def _unit_lower_inverse(a_strict):
    c = a_strict.shape[-1]
    eye = (lax.broadcasted_iota(jnp.int32, (c, c), 0) == lax.broadcasted_iota(jnp.int32, (c, c), 1))
    p = -a_strict
    m = jnp.where(eye, 1.0, 0.0) + p
    steps = int(math.log2(c)) - 1
    for _ in range(steps):
        p = _bmm(p, p)
        m = m + _bmm(m, p)
    return m


def _rwkv_chain(r, v, kk, lw, kd, b, t0, reverse):
    c = CHUNK
    row = lax.broadcasted_iota(jnp.int32, (c, c), 0)
    col = lax.broadcasted_iota(jnp.int32, (c, c), 1)
    if reverse:
        incl, strict = row <= col, row < col
        i_ref, i_last = c // 2, 0
    else:
        incl, strict = row >= col, row > col
        i_ref, i_last = c // 2 - 1, c - 1
    tri = jnp.broadcast_to(jnp.where(incl, 1.0, 0.0), (r.shape[0], c, c))
    cum = jnp.einsum('hts,hsn->htn', tri, lw, precision=HIGHEST, preferred_element_type=jnp.float32)
    ref = cum[:, i_ref:i_ref + 1, :]
    last = cum[:, i_last:i_last + 1, :]
    rel = cum - ref
    e_r = jnp.exp(rel)
    e_kk = jnp.exp(rel - lw)
    e_inv = jnp.exp(-rel)
    p_ref = jnp.exp(ref)
    p_end = jnp.exp(last - ref)
    kk_r, r_r, k_r, b_r = kk * e_kk, r * e_r, kd * e_inv, b * e_inv
    a_b = jnp.where(strict, _bmm_nt(kk_r, b_r), 0.0)
    a_k = jnp.where(strict, _bmm_nt(kk_r, k_r), 0.0)
    r_k = jnp.where(incl, _bmm_nt(r_r, k_r), 0.0)
    r_b = jnp.where(incl, _bmm_nt(r_r, b_r), 0.0)
    m = _unit_lower_inverse(a_b)
    u = _bmm(m, _bmm(kk_r * p_ref, t0) + _bmm(a_k, v))
    y = _bmm(r_r * p_ref, t0) + _bmm(r_k, v) - _bmm(r_b, u)
    p_chunk = jnp.swapaxes(jnp.broadcast_to(jnp.exp(last), (r.shape[0], r.shape[2], r.shape[2])), 1, 2)
    t1 = p_chunk * t0 + _bmm_tn(k_r * p_end, v) - _bmm_tn(b_r * p_end, u)
    return y, t1


def _rwkv_kernel(rf, vf, kkf, lwf, kdf, bf, rb, vb, kkb, lwb, kdb, bb, yf, yb, tf, tb):
    @pl.when(pl.program_id(1) == 0)
    def _():
        tf[...] = jnp.zeros_like(tf)
        tb[...] = jnp.zeros_like(tb)

    y, t1 = _rwkv_chain(rf[...], vf[...], kkf[...], lwf[...], kdf[...], bf[...], tf[...], False)
    yf[...] = y
    tf[...] = t1
    y, t1 = _rwkv_chain(rb[...], vb[...], kkb[...], lwb[...], kdb[...], bb[...], tb[...], True)
    yb[...] = y
    tb[...] = t1


def _rwkv_scan(r, v, kk, fwd, bwd, n_ctx):
    h, n, d = r.shape
    n_chunks = n // CHUNK
    ctx_chunks = n_ctx // CHUNK
    hg = RWKV_HEAD_GROUP
    blk = (hg, CHUNK, d)

    def fwd_map(g, i):
        return (g, i, 0)

    def bwd_map(g, i):
        return (g, jnp.where(i < ctx_chunks, ctx_chunks - 1 - i, n_chunks - 1 + ctx_chunks - i), 0)

    f_spec = pl.BlockSpec(blk, fwd_map)
    b_spec = pl.BlockSpec(blk, bwd_map)
    out = jax.ShapeDtypeStruct((h, n, d), jnp.float32)
    return pl.pallas_call(
        _rwkv_kernel,
        grid=(h // hg, n_chunks),
        in_specs=[f_spec] * 6 + [b_spec] * 6,
        out_specs=[f_spec, b_spec],
        out_shape=[out, out],
        scratch_shapes=[pltpu.VMEM((hg, d, d), jnp.float32), pltpu.VMEM((hg, d, d), jnp.float32)],
        compiler_params=pltpu.CompilerParams(dimension_semantics=("arbitrary", "arbitrary"),
                                             vmem_limit_bytes=VMEM_LIMIT),
        name="rwkv7_scan",
    )(r, v, kk, *fwd, r, v, kk, *bwd)


def _gla_mixer(pc, pl_, dec_w2, dec_b, norm_w):
    def prep(p):
        bsz, n = p.shape[:2]
        q, k, v, og, lr_f, lr_b = _split_cols(p, GLA_COLS)
        heads = lambda t, d: t.reshape(bsz, n, GLA_HEADS, d)
        g = [heads(jax.nn.log_sigmoid((lr @ dec_w2[d] + dec_b[d]).astype(jnp.float32)) / GLA_GATE_NORM, GLA_DK)
             for d, lr in enumerate((lr_f, lr_b))]
        return heads(q, GLA_DK) * GLA_DK ** -0.5, heads(k, GLA_DK), heads(v, GLA_DV), og, g

    qc, kc, vc, ogc, gc = prep(pc)
    ql, kl, vl, ogl, gl = prep(pl_)
    s0 = jnp.zeros((pc.shape[0], GLA_HEADS, GLA_DK, GLA_DV), pc.dtype)
    cf, lf = _two_stage(_gla_chunked, (qc, kc, vc, gc[0]), (ql, kl, vl, gl[0]), s0, False)
    cb, lb = _two_stage(_gla_chunked, (qc, kc, vc, gc[1]), (ql, kl, vl, gl[1]), s0, True)

    def finish(o, og):
        return _rms_norm(o, norm_w).reshape(og.shape) * jax.nn.silu(og)

    return finish(cf + cb, ogc), finish(lf + lb, ogl)


def _rwkv_mixer(pc, pl_, mu, w2, w0, a2, a0, g2, k_k, k_a, r_k, ln_w, ln_b):
    n_ctx = pc.shape[1]

    def prep(p):
        bsz, n = p.shape[:2]
        heads = lambda t: t.reshape(bsz, n, RWKV_HEADS, RWKV_HEAD)
        p = p + (_token_shift(p) - p) * mu
        r, k, v, wl_f, wl_b, al_f, al_b, gl = _split_cols(p, RWKV_COLS)
        kk = _l2_normalize(heads(k * k_k))
        per_dir = []
        for d, (wl, al) in enumerate(((wl_f, al_f), (wl_b, al_b))):
            z = w0[d] + jnp.tanh(wl) @ w2[d]
            log_decay = -jnp.exp(-jax.nn.softplus(-z) - 0.5)
            a = jax.nn.sigmoid(a0[d] + al @ a2[d])
            per_dir.append((heads(log_decay), heads(k * (1 + (a - 1) * k_a)), kk * heads(a)))
        g = jax.nn.sigmoid(gl) @ g2
        return heads(r), heads(k), heads(v), kk, per_dir, g

    rc, kc, vc, kkc, dc, gc = prep(pc)
    rl, kl, vl, kkl, dl, gl = prep(pl_)

    def cat(tc, tl):
        return jnp.transpose(jnp.concatenate([tc[0], tl[0]], axis=0), (1, 0, 2))

    yf, yb = _rwkv_scan(cat(rc, rl), cat(vc, vl), cat(kkc, kkl),
                        tuple(cat(tc, tl) for tc, tl in zip(dc[0], dl[0])),
                        tuple(cat(tc, tl) for tc, tl in zip(dc[1], dl[1])), n_ctx)
    y = jnp.transpose(yf + yb, (1, 0, 2))[None]

    def finish(y, r, k, v, g):
        yf = y.astype(jnp.float32)
        m = yf.mean(-1, keepdims=True)
        var = jnp.square(yf - m).mean(-1, keepdims=True)
        y = ((yf - m) * lax.rsqrt(var + RWKV_GN_EPS)).astype(y.dtype)
        y = y * ln_w.reshape(RWKV_HEADS, RWKV_HEAD) + ln_b.reshape(RWKV_HEADS, RWKV_HEAD)
        y = y + jnp.sum(r * k * r_k, axis=-1, keepdims=True) * v
        return y.reshape(g.shape) * g

    return finish(y[:, :n_ctx], rc, kc, vc, gc), finish(y[:, n_ctx:], rl, kl, vl, gl)


def _gdn_mixer(pc, pl_, conv_w, a_log, dt_bias, norm_w):
    def prep(p, on_grid):
        bsz, n = p.shape[:2]
        heads = lambda t: t.reshape(bsz, n, GDN_HEADS, GDN_HEAD)
        qkv, z, a_f, a_b, b_f, b_b = _split_cols(p, GDN_COLS)
        qkv = jax.nn.silu(_grid_conv(qkv, conv_w) if on_grid else _seq_conv(qkv, conv_w[CONV_K // 2]))
        q, k, v = jnp.split(qkv, 3, axis=-1)
        per_dir = [(-jnp.exp(a_log[d]) * jax.nn.softplus(a + dt_bias[d]), jax.nn.sigmoid(b))
                   for d, (a, b) in enumerate(((a_f, b_f), (a_b, b_b)))]
        return _l2_normalize(heads(q)) * GDN_HEAD ** -0.5, _l2_normalize(heads(k)), heads(v), z, per_dir

    qc, kc, vc, zc, dc = prep(pc, False)
    ql, kl, vl, zl, dl = prep(pl_, True)
    s0 = jnp.zeros((pc.shape[0], GDN_HEADS, GDN_HEAD, GDN_HEAD), jnp.float32)
    cf, lf = _two_stage(_gdn_chunked, (qc, kc, vc) + dc[0], (ql, kl, vl) + dl[0], s0, False)
    cb, lb = _two_stage(_gdn_chunked, (qc, kc, vc) + dc[1], (ql, kl, vl) + dl[1], s0, True)

    def finish(o, z):
        return _rms_norm(o, norm_w).reshape(z.shape) * jax.nn.silu(z)

    return finish(cf + cb, zc), finish(lf + lb, zl)


def _merge(gate_pre, outs, w_branch, w_out):
    gates = jax.nn.sigmoid(gate_pre)
    y = 0
    for m, o in enumerate(outs):
        y = y + gates[..., m * D_MODEL:(m + 1) * D_MODEL] * (o @ w_branch[m])
    return y @ w_out


def _moe_ffn(h, router_w, router_bias, w1, w3, w2):
    n_tok, d = h.shape
    scores = jax.nn.sigmoid((h @ router_w).astype(jnp.float32))
    sel = scores + router_bias.astype(jnp.float32)
    grp_top = lax.top_k(sel.reshape(n_tok, N_GROUPS, N_EXPERTS // N_GROUPS), 2)[0]
    best_group = jnp.argmax(grp_top.sum(-1), axis=-1)
    in_group = (jnp.arange(N_EXPERTS) // (N_EXPERTS // N_GROUPS))[None, :] == best_group[:, None]
    _, top_e = lax.top_k(jnp.where(in_group, sel, -jnp.inf), TOP_K)
    top_s = jnp.take_along_axis(scores, top_e, axis=-1)
    top_w = (top_s / top_s.sum(-1, keepdims=True)).astype(h.dtype)
    n_asg = n_tok * TOP_K
    flat_e = top_e.reshape(-1)
    order = jnp.argsort(flat_e)
    sorted_e = flat_e[order]
    counts = jnp.bincount(flat_e, length=N_EXPERTS)
    padded = (counts + DISPATCH_BLOCK - 1) // DISPATCH_BLOCK * DISPATCH_BLOCK
    pad_end = jnp.cumsum(padded)
    first = jnp.cumsum(counts) - counts
    dest = (pad_end - padded)[sorted_e] + jnp.arange(n_asg) - first[sorted_e]
    n_blk = -(-n_asg // DISPATCH_BLOCK) + N_EXPERTS
    tok = jnp.repeat(jnp.arange(n_tok, dtype=jnp.int32), TOP_K)
    slot_tok = jnp.full((n_blk * DISPATCH_BLOCK,), n_tok, jnp.int32).at[dest].set(tok[order])
    slot_w = jnp.zeros((n_blk * DISPATCH_BLOCK,), h.dtype).at[dest].set(top_w.reshape(-1)[order])
    blk_e = jnp.minimum(jnp.searchsorted(pad_end, jnp.arange(n_blk) * DISPATCH_BLOCK, side='right'), N_EXPERTS - 1)
    h_pad = jnp.concatenate([h, jnp.zeros((1, d), h.dtype)], axis=0)
    xb = h_pad[slot_tok].reshape(n_blk, DISPATCH_BLOCK, d)

    def expert_block(args):
        xe, e = args
        return (jax.nn.silu(xe @ w1[e]) * (xe @ w3[e])) @ w2[e]

    yb = lax.map(expert_block, (xb, blk_e))
    y = jnp.zeros((n_tok + 1, d), h.dtype).at[slot_tok].add(yb.reshape(-1, d) * slot_w[:, None])
    return y[:n_tok]


def kernel(x, c, ctx, c_ctx, w_ada, b_ada, w_in, b_in,
           gla_dec_w, gla_dec_b, gla_norm_w,
           rwkv_mu, rwkv_w2, rwkv_w0, rwkv_a2, rwkv_a0, rwkv_g2, rwkv_kk, rwkv_ka, rwkv_rk,
           rwkv_ln_w, rwkv_ln_b,
           gdn_conv_w, gdn_a_log, gdn_dt_bias, gdn_norm_w,
           w_branch, w_out, ln1_g, ln1_b,
           router_w, router_bias, moe_w1, moe_w3, moe_w2, ln2_g, ln2_b):
    alpha = (2.0 * DEPTH) ** 0.25
    bsz, n_lat, d = x.shape
    n_ctx = ctx.shape[1]
    silu_c = jax.nn.silu(c)[:, None, :]
    silu_cc = jax.nn.silu(c_ctx)
    xl, xc = x, ctx
    for l in range(DEPTH):
        last = l == DEPTH - 1
        mod_l = jnp.split(silu_c @ w_ada[l] + b_ada[l], 6, axis=-1)
        mod_c = jnp.split(silu_cc @ w_ada[l] + b_ada[l], 6, axis=-1)
        hl = xl * (1 + mod_l[1]) + mod_l[0]
        hc = xc * (1 + mod_c[1]) + mod_c[0]
        pl_ = hl @ w_in[l] + b_in[l]
        pc = hc @ w_in[l] + b_in[l]
        gla_c, rwkv_c, gdn_c, gate_c = _split_cols(pc, IN_COLS)
        gla_l, rwkv_l, gdn_l, gate_l = _split_cols(pl_, IN_COLS)
        oa_c, oa_l = _gla_mixer(gla_c, gla_l, gla_dec_w[l], gla_dec_b[l], gla_norm_w[l])
        ob_c, ob_l = _rwkv_mixer(rwkv_c, rwkv_l, rwkv_mu[l], rwkv_w2[l], rwkv_w0[l], rwkv_a2[l], rwkv_a0[l],
                                 rwkv_g2[l], rwkv_kk[l], rwkv_ka[l], rwkv_rk[l], rwkv_ln_w[l], rwkv_ln_b[l])
        oc_c, oc_l = _gdn_mixer(gdn_c, gdn_l, gdn_conv_w[l], gdn_a_log[l], gdn_dt_bias[l], gdn_norm_w[l])
        mix_l = _merge(gate_l, (oa_l, ob_l, oc_l), w_branch[l], w_out[l])
        xl = _layer_norm(alpha * xl + mod_l[2] * mix_l, ln1_g[l], ln1_b[l])
        h2l = xl * (1 + mod_l[4]) + mod_l[3]
        if last:
            yl = _moe_ffn(h2l.reshape(-1, d), router_w, router_bias,
                          moe_w1[l], moe_w3[l], moe_w2[l]).reshape(xl.shape)
        else:
            mix_c = _merge(gate_c, (oa_c, ob_c, oc_c), w_branch[l], w_out[l])
            xc = _layer_norm(alpha * xc + mod_c[2] * mix_c, ln1_g[l], ln1_b[l])
            h2c = xc * (1 + mod_c[4]) + mod_c[3]
            y = _moe_ffn(jnp.concatenate([h2c, h2l], axis=1).reshape(-1, d), router_w, router_bias,
                         moe_w1[l], moe_w3[l], moe_w2[l]).reshape(bsz, n_ctx + n_lat, d)
            xc = _layer_norm(alpha * xc + mod_c[5] * y[:, :n_ctx], ln2_g[l], ln2_b[l])
            yl = y[:, n_ctx:]
        xl = _layer_norm(alpha * xl + mod_l[5] * yl, ln2_g[l], ln2_b[l])
    return xl
```

```python
import functools
import math

import jax
import jax.numpy as jnp
from jax import lax
from jax.experimental import pallas as pl
from jax.experimental.pallas import tpu as pltpu

D_MODEL = 2048
DEPTH = 4
GRID_W = 64
MIX_W = D_MODEL // 2
CHUNK = 64
GLA_HEADS = 4
GLA_DV = MIX_W // GLA_HEADS
GLA_DK = GLA_DV // 2
GLA_LR = 16
GLA_GATE_NORM = 16.0
RWKV_HEAD = 64
RWKV_HEADS = MIX_W // RWKV_HEAD
DECAY_LORA = 64
ICLR_LORA = 64
GATE_LORA = 128
RWKV_GN_EPS = 64e-5
GDN_HEAD = 128
GDN_HEADS = MIX_W // GDN_HEAD
CONV_K = 3
N_EXPERTS = 16
N_GROUPS = 4
TOP_K = 2
D_EXPERT = 1408
DISPATCH_BLOCK = 256

GLA_COLS = (GLA_HEADS * GLA_DK, GLA_HEADS * GLA_DK, MIX_W, MIX_W, GLA_LR, GLA_LR)
RWKV_COLS = (MIX_W, MIX_W, MIX_W, DECAY_LORA, DECAY_LORA, ICLR_LORA, ICLR_LORA, GATE_LORA)
GDN_COLS = (3 * MIX_W, MIX_W, GDN_HEADS, GDN_HEADS, GDN_HEADS, GDN_HEADS)
GLA_PROJ = sum(GLA_COLS)
RWKV_PROJ = sum(RWKV_COLS)
GDN_PROJ = sum(GDN_COLS)

LANE = 128
VMEM_LIMIT = 56 * 1024 * 1024
HIGHEST = lax.Precision.HIGHEST
NEG_BIG = -1e30


def _bf(x):
    return x.astype(jnp.bfloat16)


def _mm(a, b):
    return jnp.matmul(a, b)


def _split_cols(p, widths):
    out, s = [], 0
    for w in widths:
        out.append(p[..., s:s + w])
        s += w
    return out


def _layer_norm(x, g, b, eps=1e-5):
    xf = x.astype(jnp.float32)
    mu = xf.mean(-1, keepdims=True)
    var = jnp.square(xf - mu).mean(-1, keepdims=True)
    return ((xf - mu) * lax.rsqrt(var + eps)).astype(x.dtype) * g + b


def _rms_norm(x, w, eps=1e-6):
    xf = x.astype(jnp.float32)
    return (xf * lax.rsqrt(jnp.square(xf).mean(-1, keepdims=True) + eps)).astype(x.dtype) * w


def _l2_normalize(x, eps=1e-6):
    xf = x.astype(jnp.float32)
    return (xf * lax.rsqrt(jnp.square(xf).sum(-1, keepdims=True) + eps)).astype(x.dtype)


def _token_shift(p):
    pad = jnp.pad(p, ((0, 0), (1, 1), (0, 0)))
    return 0.5 * (pad[:, :-2] + pad[:, 2:])


def _grid_conv(t, w):
    b, n, ch = t.shape
    rows = n // GRID_W
    y = lax.conv_general_dilated(t.reshape(b, rows, GRID_W, ch), w[:, :, None, :], (1, 1), 'SAME',
                                 dimension_numbers=('NHWC', 'HWIO', 'NHWC'), feature_group_count=ch)
    return y.reshape(b, n, ch)


def _seq_conv(t, w):
    return lax.conv_general_dilated(t, w[:, None, :], (1,), 'SAME',
                                    dimension_numbers=('NWC', 'WIO', 'NWC'), feature_group_count=t.shape[-1])


def _bmm(a, b):
    return jnp.einsum('hmk,hkn->hmn', _bf(a), _bf(b), preferred_element_type=jnp.float32)


def _bmm_nt(a, b):
    return jnp.einsum('hmk,hnk->hmn', _bf(a), _bf(b), preferred_element_type=jnp.float32)


def _bmm_tn(a, b):
    return jnp.einsum('hkm,hkn->hmn', _bf(a), _bf(b), preferred_element_type=jnp.float32)


def _chunk_masks(reverse):
    c = CHUNK
    row = lax.broadcasted_iota(jnp.int32, (c, c), 0)
    col = lax.broadcasted_iota(jnp.int32, (c, c), 1)
    if reverse:
        return row <= col, row < col, c // 2, 0
    return row >= col, row > col, c // 2 - 1, c - 1


def _running_sum(x, incl):
    tri = jnp.broadcast_to(jnp.where(incl, 1.0, 0.0), (x.shape[0], CHUNK, CHUNK))
    return jnp.einsum('hts,hsn->htn', tri, x, precision=HIGHEST, preferred_element_type=jnp.float32)


def _unit_triangular_inverse(a_strict):
    c = a_strict.shape[-1]
    eye = (lax.broadcasted_iota(jnp.int32, (c, c), 0) == lax.broadcasted_iota(jnp.int32, (c, c), 1))
    p = -a_strict
    m = jnp.where(eye, 1.0, 0.0) + p
    for _ in range(int(math.log2(c)) - 1):
        p = _bmm(p, p)
        m = m + _bmm(m, p)
    return m


def _column_of(row_vec, n):
    return jnp.swapaxes(jnp.broadcast_to(row_vec, (row_vec.shape[0], n, n)), 1, 2)


def _rwkv_chain(shared, per_dir, t0, reverse):
    r, v, kk = shared
    lw, kd, b = per_dir
    incl, strict, i_ref, i_last = _chunk_masks(reverse)
    cum = _running_sum(lw, incl)
    ref = cum[:, i_ref:i_ref + 1, :]
    last = cum[:, i_last:i_last + 1, :]
    rel = cum - ref
    e_inv = jnp.exp(-rel)
    p_ref = jnp.exp(ref)
    p_end = jnp.exp(last - ref)
    kk_r, r_r, k_r, b_r = kk * jnp.exp(rel - lw), r * jnp.exp(rel), kd * e_inv, b * e_inv
    a_b = jnp.where(strict, _bmm_nt(kk_r, b_r), 0.0)
    a_k = jnp.where(strict, _bmm_nt(kk_r, k_r), 0.0)
    r_k = jnp.where(incl, _bmm_nt(r_r, k_r), 0.0)
    r_b = jnp.where(incl, _bmm_nt(r_r, b_r), 0.0)
    m = _unit_triangular_inverse(a_b)
    u = _bmm(m, _bmm(kk_r * p_ref, t0) + _bmm(a_k, v))
    y = _bmm(r_r * p_ref, t0) + _bmm(r_k, v) - _bmm(r_b, u)
    t1 = _column_of(jnp.exp(last), r.shape[2]) * t0 + _bmm_tn(k_r * p_end, v) - _bmm_tn(b_r * p_end, u)
    return y, t1


def _gla_chain(shared, per_dir, s0, reverse):
    q, k, v = shared
    (g,) = per_dir
    incl, _, i_ref, i_last = _chunk_masks(reverse)
    cum = _running_sum(g, incl)
    ref = cum[:, i_ref:i_ref + 1, :]
    last = cum[:, i_last:i_last + 1, :]
    att = jnp.where(incl, _bmm_nt(q * jnp.exp(cum - ref), k * jnp.exp(ref - cum)), 0.0)
    o = _bmm(att, v) + _bmm(q * jnp.exp(cum), s0)
    dk, dv = s0.shape[1], s0.shape[2]
    dec = _column_of(jnp.exp(last), dk)
    s1 = jnp.concatenate([dec] * (dv // dk), axis=-1) * s0 + _bmm_tn(k * jnp.exp(last - cum), v)
    return o, s1


def _gdn_chain(shared, per_dir, s0, reverse):
    q, k, v = shared
    la, beta = per_dir
    c = CHUNK
    incl, strict, _, i_last = _chunk_masks(reverse)
    gam = _running_sum(la, incl)
    last = gam[:, i_last:i_last + 1, :]
    gcol = gam[:, :, :c]
    decay = jnp.exp(jnp.where(incl, gcol - jnp.swapaxes(gcol, 1, 2), NEG_BIG))
    e_gam = jnp.exp(gam)
    a_mat = jnp.where(strict, beta[:, :, :c] * _bmm_nt(k, k) * decay, 0.0)
    rhs = jnp.concatenate([v * beta, k * (beta * e_gam)], axis=-1)
    sol = _bmm(_unit_triangular_inverse(a_mat), rhs)
    dv = v.shape[-1]
    u, w = sol[..., :dv], sol[..., dv:]
    qk = _bmm_nt(q, k) * decay
    delta = u - _bmm(w, s0)
    o = _bmm(q * e_gam, s0) + _bmm(qk, delta)
    s1 = jnp.exp(last) * s0 + _bmm_tn(k * jnp.exp(last - gam), delta)
    return o, s1


def _scan_kernel(*refs, chain, n_shared, n_dir):
    n_in = n_shared + n_dir
    f_in, b_in = refs[:n_in], refs[n_in:2 * n_in]
    yf, yb, sf, sb = refs[2 * n_in:]

    @pl.when(pl.program_id(1) == 0)
    def _():
        sf[...] = jnp.zeros_like(sf)
        sb[...] = jnp.zeros_like(sb)

    for ins, y, s, reverse in ((f_in, yf, sf, False), (b_in, yb, sb, True)):
        vals = [t[...] for t in ins]
        out, s1 = chain(vals[:n_shared], vals[n_shared:], s[...], reverse)
        y[...] = out
        s[...] = s1


def _bidir_scan(chain, shared, fwd, bwd, out_dim, state_dims, head_group, n_ctx, name):
    h, n, _ = shared[0].shape
    n_chunks = n // CHUNK
    ctx_chunks = n_ctx // CHUNK

    def fwd_map(g, i):
        return (g, i, 0)

    def bwd_map(g, i):
        return (g, jnp.where(i < ctx_chunks, ctx_chunks - 1 - i, n_chunks - 1 + ctx_chunks - i), 0)

    def specs(arrays, index_map):
        return [pl.BlockSpec((head_group, CHUNK, a.shape[2]), index_map) for a in arrays]

    ins = list(shared) + list(fwd)
    out = jax.ShapeDtypeStruct((h, n, out_dim), jnp.float32)
    state = pltpu.VMEM((head_group,) + state_dims, jnp.float32)
    return pl.pallas_call(
        functools.partial(_scan_kernel, chain=chain, n_shared=len(shared), n_dir=len(fwd)),
        grid=(h // head_group, n_chunks),
        in_specs=specs(ins, fwd_map) + specs(ins, bwd_map),
        out_specs=[pl.BlockSpec((head_group, CHUNK, out_dim), fwd_map),
                   pl.BlockSpec((head_group, CHUNK, out_dim), bwd_map)],
        out_shape=[out, out],
        scratch_shapes=[state, state],
        compiler_params=pltpu.CompilerParams(dimension_semantics=("arbitrary", "arbitrary"),
                                             vmem_limit_bytes=VMEM_LIMIT),
        name=name,
    )(*shared, *fwd, *shared, *bwd)


def _head_major(tc, tl):
    return jnp.transpose(jnp.concatenate([tc[0], tl[0]], axis=0), (1, 0, 2))


def _token_major(y, n_ctx):
    y = jnp.transpose(y, (1, 0, 2))[None]
    return y[:, :n_ctx], y[:, n_ctx:]


SEG_ROWS = 256
MM_TM = 1056
IN_TN = 896


def _modulate_kernel(x, scale, shift, o):
    o[...] = (x[...] * (1.0 + scale[0]) + shift[0]).astype(o.dtype)


def _modulate(x, scale, shift, n_ctx):
    n, d = x.shape
    assert n_ctx == SEG_ROWS and n % SEG_ROWS == 0
    seg = lambda i: (jnp.minimum(i, 1), 0, 0)
    return pl.pallas_call(
        _modulate_kernel,
        grid=(n // SEG_ROWS,),
        in_specs=[pl.BlockSpec((SEG_ROWS, d), lambda i: (i, 0)),
                  pl.BlockSpec((1, 1, d), seg), pl.BlockSpec((1, 1, d), seg)],
        out_specs=pl.BlockSpec((SEG_ROWS, d), lambda i: (i, 0)),
        out_shape=jax.ShapeDtypeStruct((n, d), jnp.bfloat16),
        compiler_params=pltpu.CompilerParams(dimension_semantics=("arbitrary",), vmem_limit_bytes=VMEM_LIMIT),
        name="modulate",
    )(x, scale[:, None, :], shift[:, None, :])


def _matmul_bias_kernel(x, w, b, o):
    o[...] = jnp.dot(x[...], w[...], preferred_element_type=jnp.float32) + b[...]


def _matmul_bias(x, w, b, tm, tn):
    m, k = x.shape
    n = w.shape[1]
    assert m % tm == 0 and n % tn == 0
    return pl.pallas_call(
        _matmul_bias_kernel,
        grid=(m // tm, n // tn),
        in_specs=[pl.BlockSpec((tm, k), lambda i, j: (i, 0)),
                  pl.BlockSpec((k, tn), lambda i, j: (0, j)),
                  pl.BlockSpec((1, tn), lambda i, j: (0, j))],
        out_specs=pl.BlockSpec((tm, tn), lambda i, j: (i, j)),
        out_shape=jax.ShapeDtypeStruct((m, n), jnp.float32),
        compiler_params=pltpu.CompilerParams(dimension_semantics=("arbitrary", "arbitrary"),
                                             vmem_limit_bytes=VMEM_LIMIT),
        name="matmul_bias",
    )(x, w, b)


_GLA_MAIN = GLA_PROJ - 2 * GLA_LR
_GDN_MAIN = GDN_PROJ - 4 * GDN_HEADS
_LR_PAD = LANE - 2 * GLA_LR
_AB_PAD = LANE - 4 * GDN_HEADS
OFF_GLA = 0
OFF_RWKV = _GLA_MAIN + LANE
OFF_GDN = OFF_RWKV + RWKV_PROJ
OFF_GATE = OFF_GDN + _GDN_MAIN + LANE
IN_PAD = OFF_GATE + 3 * D_MODEL


def _pad_in_cols(w):
    o_rwkv, o_gdn, o_gate = GLA_PROJ, GLA_PROJ + RWKV_PROJ, GLA_PROJ + RWKV_PROJ + GDN_PROJ
    z = lambda n: jnp.zeros(w.shape[:-1] + (n,), w.dtype)
    return jnp.concatenate([w[..., :o_rwkv], z(_LR_PAD), w[..., o_rwkv:o_gdn], w[..., o_gdn:o_gate], z(_AB_PAD),
                            w[..., o_gate:]], axis=-1)


def _gla_mixer(pc, pl_, dec_w2, dec_b, norm_w):
    n_ctx = pc.shape[1]

    def prep(p):
        bsz, n = p.shape[:2]
        q, k, v, og, lr_f, lr_b = _split_cols(p, GLA_COLS)
        heads = lambda t, d: t.reshape(bsz, n, GLA_HEADS, d)
        g = [heads(jax.nn.log_sigmoid((_mm(lr, dec_w2[d]) + dec_b[d]).astype(jnp.float32)) / GLA_GATE_NORM, GLA_DK)
             for d, lr in enumerate((lr_f, lr_b))]
        return heads(q, GLA_DK) * GLA_DK ** -0.5, heads(k, GLA_DK), heads(v, GLA_DV), og, g

    qc, kc, vc, ogc, gc = prep(pc)
    ql, kl, vl, ogl, gl = prep(pl_)
    of, ob = _bidir_scan(_gla_chain, (_head_major(qc, ql), _head_major(kc, kl), _head_major(vc, vl)),
                         (_head_major(gc[0], gl[0]),), (_head_major(gc[1], gl[1]),),
                         GLA_DV, (GLA_DK, GLA_DV), GLA_HEADS, n_ctx, "gla_scan")
    oc, ol = _token_major(of + ob, n_ctx)

    def finish(o, og):
        return _rms_norm(o, norm_w).reshape(og.shape) * jax.nn.silu(og)

    return finish(oc, ogc), finish(ol, ogl)


RWKV_HEAD_GROUP = 8


def _rwkv_mixer(pc, pl_, mu, w2, w0, a2, a0, g2, k_k, k_a, r_k, ln_w, ln_b):
    n_ctx = pc.shape[1]

    def prep(p):
        bsz, n = p.shape[:2]
        heads = lambda t: t.reshape(bsz, n, RWKV_HEADS, RWKV_HEAD)
        p = p + (_token_shift(p) - p) * mu
        r, k, v, wl_f, wl_b, al_f, al_b, gl = _split_cols(p, RWKV_COLS)
        kk = _l2_normalize(heads(k * k_k))
        per_dir = []
        for d, (wl, al) in enumerate(((wl_f, al_f), (wl_b, al_b))):
            z = w0[d] + _mm(jnp.tanh(wl), w2[d])
            log_decay = -jnp.exp(-jax.nn.softplus(-z) - 0.5)
            a = jax.nn.sigmoid(a0[d] + _mm(al, a2[d]))
            per_dir.append((heads(log_decay), heads(k * (1 + (a - 1) * k_a)), kk * heads(a)))
        g = _mm(jax.nn.sigmoid(gl), g2)
        return heads(r), heads(k), heads(v), kk, per_dir, g

    rc, kc, vc, kkc, dc, gc = prep(pc)
    rl, kl, vl, kkl, dl, gl = prep(pl_)
    yf, yb = _bidir_scan(_rwkv_chain, (_head_major(rc, rl), _head_major(vc, vl), _head_major(kkc, kkl)),
                         tuple(_head_major(tc, tl) for tc, tl in zip(dc[0], dl[0])),
                         tuple(_head_major(tc, tl) for tc, tl in zip(dc[1], dl[1])),
                         RWKV_HEAD, (RWKV_HEAD, RWKV_HEAD), RWKV_HEAD_GROUP, n_ctx, "rwkv7_scan")
    yc, yl = _token_major(yf + yb, n_ctx)

    def finish(y, r, k, v, g):
        yf = y.astype(jnp.float32)
        m = yf.mean(-1, keepdims=True)
        var = jnp.square(yf - m).mean(-1, keepdims=True)
        y = ((yf - m) * lax.rsqrt(var + RWKV_GN_EPS)).astype(y.dtype)
        y = y * ln_w.reshape(RWKV_HEADS, RWKV_HEAD) + ln_b.reshape(RWKV_HEADS, RWKV_HEAD)
        y = y + jnp.sum(r * k * r_k, axis=-1, keepdims=True) * v
        return y.reshape(g.shape) * g

    return finish(yc, rc, kc, vc, gc), finish(yl, rl, kl, vl, gl)


GDN_DEFAULT_PRECISION_WEIGHT = 0.75


def _to_chunks(t):
    b, n = t.shape[:2]
    t = t.reshape((b, n // CHUNK, CHUNK) + t.shape[2:])
    return jnp.swapaxes(t, 2, 3)


def _from_chunks(t):
    t = jnp.swapaxes(t, 2, 3)
    return t.reshape((t.shape[0], t.shape[1] * t.shape[2]) + t.shape[3:])


def _two_stage(scan_fn, ctx_args, lat_args, state0, reverse):
    if reverse:
        ctx_args = tuple(jnp.flip(t, axis=1) for t in ctx_args)
        lat_args = tuple(jnp.flip(t, axis=1) for t in lat_args)
    o_ctx, s_ctx = scan_fn(*ctx_args, state0)
    o_lat, _ = scan_fn(*lat_args, s_ctx)
    if reverse:
        o_ctx, o_lat = jnp.flip(o_ctx, axis=1), jnp.flip(o_lat, axis=1)
    return o_ctx, o_lat


def _gdn_chunked(q, k, v, log_a, beta, s0):
    dt = v.dtype
    f32 = jnp.float32
    q, k, v, log_a, beta = (_to_chunks(t.astype(f32)) for t in (q, k, v, log_a, beta))
    gam = jnp.cumsum(log_a, axis=-1)
    incl = jnp.tril(jnp.ones((CHUNK, CHUNK), bool))
    strict = jnp.tril(jnp.ones((CHUNK, CHUNK), bool), -1)
    decay = jnp.exp(jnp.where(incl, gam[..., :, None] - gam[..., None, :], -jnp.inf))
    a_mat = jnp.where(strict, beta[..., :, None] * jnp.einsum('bnhid,bnhjd->bnhij', k, k) * decay, 0.0)
    rhs = jnp.concatenate([v * beta[..., None], k * (beta * jnp.exp(gam))[..., None]], axis=-1)
    sol = lax.linalg.triangular_solve(a_mat, rhs, left_side=True, lower=True, unit_diagonal=True)
    dv = v.shape[-1]
    u, w = sol[..., :dv], sol[..., dv:]
    qk = jnp.einsum('bnhid,bnhjd->bnhij', q, k) * decay
    q_dec = q * jnp.exp(gam)[..., None]
    k_end = k * jnp.exp(gam[..., -1:] - gam)[..., None]
    dec = jnp.exp(gam[..., -1])

    def step(s, inp):
        u_n, w_n, qk_n, q_n, k_n, dec_n = inp
        delta = u_n - jnp.einsum('bhcd,bhdv->bhcv', w_n, s)
        o_n = jnp.einsum('bhcd,bhdv->bhcv', q_n, s) + jnp.einsum('bhij,bhjv->bhiv', qk_n, delta)
        s = dec_n[..., None, None] * s + jnp.einsum('bhcd,bhcv->bhdv', k_n, delta)
        return s, o_n

    xs = tuple(jnp.moveaxis(t, 1, 0) for t in (u, w, qk, q_dec, k_end, dec))
    s_fin, o = lax.scan(step, s0, xs)
    return _from_chunks(jnp.moveaxis(o, 0, 1)).astype(dt), s_fin


def _gdn_mixer(pc, pl_, conv_w, a_log, dt_bias, norm_w):
    n_ctx = pc.shape[1]

    def prep(p, on_grid):
        bsz, n = p.shape[:2]
        heads = lambda t: t.reshape(bsz, n, GDN_HEADS, GDN_HEAD)
        qkv, z, a_f, a_b, b_f, b_b = _split_cols(p, GDN_COLS)
        qkv = jax.nn.silu(_grid_conv(qkv, conv_w) if on_grid else _seq_conv(qkv, conv_w[CONV_K // 2]))
        q, k, v = jnp.split(qkv, 3, axis=-1)
        per_dir = [(-jnp.exp(a_log[d]) * jax.nn.softplus(a + dt_bias[d]), jax.nn.sigmoid(b))
                   for d, (a, b) in enumerate(((a_f, b_f), (a_b, b_b)))]
        return _l2_normalize(heads(q)) * GDN_HEAD ** -0.5, _l2_normalize(heads(k)), heads(v), z, per_dir

    qc, kc, vc, zc, dc = prep(pc, False)
    ql, kl, vl, zl, dl = prep(pl_, True)

    def lanes(tc, tl):
        t = jnp.transpose(jnp.concatenate([tc[0], tl[0]], axis=0), (1, 0))
        return jnp.broadcast_to(t[:, :, None], t.shape + (GDN_HEAD,))

    of, ob = _bidir_scan(_gdn_chain, (_head_major(qc, ql), _head_major(kc, kl), _head_major(vc, vl)),
                         tuple(lanes(tc, tl) for tc, tl in zip(dc[0], dl[0])),
                         tuple(lanes(tc, tl) for tc, tl in zip(dc[1], dl[1])),
                         GDN_HEAD, (GDN_HEAD, GDN_HEAD), GDN_HEADS, n_ctx, "gdn_scan")
    pc_o, pl_o = _token_major(of + ob, n_ctx)
    s0 = jnp.zeros((pc.shape[0], GDN_HEADS, GDN_HEAD, GDN_HEAD), jnp.float32)
    cf, lf = _two_stage(_gdn_chunked, (qc, kc, vc) + dc[0], (ql, kl, vl) + dl[0], s0, False)
    cb, lb = _two_stage(_gdn_chunked, (qc, kc, vc) + dc[1], (ql, kl, vl) + dl[1], s0, True)
    mix = lambda a, b: GDN_DEFAULT_PRECISION_WEIGHT * a + (1.0 - GDN_DEFAULT_PRECISION_WEIGHT) * b
    oc, ol = mix(cf + cb, pc_o), mix(lf + lb, pl_o)

    def finish(o, z):
        return _rms_norm(o, norm_w).reshape(z.shape) * jax.nn.silu(z)

    return finish(oc, zc), finish(ol, zl)


def _merge(gate_pre, outs, w_branch, w_out):
    gates = jax.nn.sigmoid(gate_pre)
    y = 0
    for m, o in enumerate(outs):
        y = y + gates[..., m * D_MODEL:(m + 1) * D_MODEL] * _mm(o, w_branch[m])
    return _mm(y, w_out)


def _expert_kernel(blk_e, n_used, x, w1, w3, w2, o):
    i = pl.program_id(0)

    @pl.when(i < n_used[0])
    def _():
        xe = x[...]
        h1 = jnp.dot(xe, w1[0], preferred_element_type=jnp.float32)
        h3 = jnp.dot(xe, w3[0], preferred_element_type=jnp.float32)
        o[...] = jnp.dot(_bf(jax.nn.silu(h1) * h3), w2[0], preferred_element_type=jnp.float32)

    @pl.when(i >= n_used[0])
    def _():
        o[...] = jnp.zeros_like(o)


def _expert_blocks(blk_e, n_used, xb, w1, w3, w2):
    n_slots, d = xb.shape
    de = w1.shape[2]
    n_blk = n_slots // DISPATCH_BLOCK
    grid_spec = pltpu.PrefetchScalarGridSpec(
        num_scalar_prefetch=2,
        grid=(n_blk,),
        in_specs=[pl.BlockSpec((DISPATCH_BLOCK, d), lambda i, be, nu: (i, 0)),
                  pl.BlockSpec((1, d, de), lambda i, be, nu: (be[i], 0, 0)),
                  pl.BlockSpec((1, d, de), lambda i, be, nu: (be[i], 0, 0)),
                  pl.BlockSpec((1, de, d), lambda i, be, nu: (be[i], 0, 0))],
        out_specs=pl.BlockSpec((DISPATCH_BLOCK, d), lambda i, be, nu: (i, 0)),
    )
    return pl.pallas_call(
        _expert_kernel,
        grid_spec=grid_spec,
        out_shape=jax.ShapeDtypeStruct((n_slots, d), jnp.float32),
        compiler_params=pltpu.CompilerParams(dimension_semantics=("arbitrary",), vmem_limit_bytes=VMEM_LIMIT),
        name="moe_experts",
    )(blk_e, n_used, xb, w1, w3, w2)


def _moe_ffn(h, router_w, router_bias, w1, w3, w2):
    n_tok, d = h.shape
    scores = jax.nn.sigmoid(_mm(h, router_w))
    sel = scores + router_bias.astype(jnp.float32)
    grp_top = lax.top_k(sel.reshape(n_tok, N_GROUPS, N_EXPERTS // N_GROUPS), 2)[0]
    best_group = jnp.argmax(grp_top.sum(-1), axis=-1)
    in_group = (jnp.arange(N_EXPERTS) // (N_EXPERTS // N_GROUPS))[None, :] == best_group[:, None]
    _, top_e = lax.top_k(jnp.where(in_group, sel, -jnp.inf), TOP_K)
    top_s = jnp.take_along_axis(scores, top_e, axis=-1)
    top_w = (top_s / top_s.sum(-1, keepdims=True)).astype(h.dtype)
    n_asg = n_tok * TOP_K
    flat_e = top_e.reshape(-1)
    order = jnp.argsort(flat_e)
    sorted_e = flat_e[order]
    counts = jnp.bincount(flat_e, length=N_EXPERTS)
    padded = (counts + DISPATCH_BLOCK - 1) // DISPATCH_BLOCK * DISPATCH_BLOCK
    pad_end = jnp.cumsum(padded)
    first = jnp.cumsum(counts) - counts
    dest = (pad_end - padded)[sorted_e] + jnp.arange(n_asg) - first[sorted_e]
    n_blk = -(-n_asg // DISPATCH_BLOCK) + N_EXPERTS
    tok = jnp.repeat(jnp.arange(n_tok, dtype=jnp.int32), TOP_K)
    slot_tok = jnp.full((n_blk * DISPATCH_BLOCK,), n_tok, jnp.int32).at[dest].set(tok[order])
    blk_e = jnp.minimum(jnp.searchsorted(pad_end, jnp.arange(n_blk) * DISPATCH_BLOCK, side='right'), N_EXPERTS - 1)
    h_pad = jnp.concatenate([_bf(h), jnp.zeros((1, d), jnp.bfloat16)], axis=0)
    n_used = (pad_end[-1] // DISPATCH_BLOCK).astype(jnp.int32).reshape(1)
    yb = _expert_blocks(blk_e.astype(jnp.int32), n_used, h_pad[slot_tok], _bf(w1), _bf(w3), _bf(w2))
    slot_of = jnp.zeros((n_asg,), jnp.int32).at[order].set(dest.astype(jnp.int32)).reshape(n_tok, TOP_K)
    y = 0
    for j in range(TOP_K):
        y = y + yb[slot_of[:, j]] * top_w[:, j:j + 1]
    return y


def kernel(x, c, ctx, c_ctx, w_ada, b_ada, w_in, b_in,
           gla_dec_w, gla_dec_b, gla_norm_w,
           rwkv_mu, rwkv_w2, rwkv_w0, rwkv_a2, rwkv_a0, rwkv_g2, rwkv_kk, rwkv_ka, rwkv_rk,
           rwkv_ln_w, rwkv_ln_b,
           gdn_conv_w, gdn_a_log, gdn_dt_bias, gdn_norm_w,
           w_branch, w_out, ln1_g, ln1_b,
           router_w, router_bias, moe_w1, moe_w3, moe_w2, ln2_g, ln2_b):
    alpha = (2.0 * DEPTH) ** 0.25
    bsz, n_lat, d = x.shape
    n_ctx = ctx.shape[1]
    silu_c = jax.nn.silu(c)[:, None, :]
    silu_cc = jax.nn.silu(c_ctx)
    xl, xc = x, ctx
    for l in range(DEPTH):
        last = l == DEPTH - 1
        mod_l = jnp.split(_mm(silu_c, w_ada[l]) + b_ada[l], 6, axis=-1)
        mod_c = jnp.split(_mm(silu_cc, w_ada[l]) + b_ada[l], 6, axis=-1)
        h = _modulate(jnp.concatenate([xc[0], xl[0]], axis=0),
                      jnp.stack([mod_c[1], mod_l[1][0, 0]]), jnp.stack([mod_c[0], mod_l[0][0, 0]]), n_ctx)
        p = _matmul_bias(h, _bf(_pad_in_cols(w_in[l])), _pad_in_cols(b_in[l])[None], MM_TM, IN_TN)
        slab = lambda off, width: (p[None, :n_ctx, off:off + width], p[None, n_ctx:, off:off + width])
        gla_c, gla_l = slab(OFF_GLA, GLA_PROJ)
        rwkv_c, rwkv_l = slab(OFF_RWKV, RWKV_PROJ)
        gdn_c, gdn_l = slab(OFF_GDN, GDN_PROJ)
        gate_c, gate_l = slab(OFF_GATE, 3 * D_MODEL)
        oa_c, oa_l = _gla_mixer(gla_c, gla_l, gla_dec_w[l], gla_dec_b[l], gla_norm_w[l])
        ob_c, ob_l = _rwkv_mixer(rwkv_c, rwkv_l, rwkv_mu[l], rwkv_w2[l], rwkv_w0[l], rwkv_a2[l], rwkv_a0[l],
                                 rwkv_g2[l], rwkv_kk[l], rwkv_ka[l], rwkv_rk[l], rwkv_ln_w[l], rwkv_ln_b[l])
        oc_c, oc_l = _gdn_mixer(gdn_c, gdn_l, gdn_conv_w[l], gdn_a_log[l], gdn_dt_bias[l], gdn_norm_w[l])
        mix_l = _merge(gate_l, (oa_l, ob_l, oc_l), w_branch[l], w_out[l])
        xl = _layer_norm(alpha * xl + mod_l[2] * mix_l, ln1_g[l], ln1_b[l])
        h2l = xl * (1 + mod_l[4]) + mod_l[3]
        if last:
            yl = _moe_ffn(h2l.reshape(-1, d), router_w, router_bias,
                          moe_w1[l], moe_w3[l], moe_w2[l]).reshape(xl.shape)
        else:
            mix_c = _merge(gate_c, (oa_c, ob_c, oc_c), w_branch[l], w_out[l])
            xc = _layer_norm(alpha * xc + mod_c[2] * mix_c, ln1_g[l], ln1_b[l])
            h2c = xc * (1 + mod_c[4]) + mod_c[3]
            y = _moe_ffn(jnp.concatenate([h2c, h2l], axis=1).reshape(-1, d), router_w, router_bias,
                         moe_w1[l], moe_w3[l], moe_w2[l]).reshape(bsz, n_ctx + n_lat, d)
            xc = _layer_norm(alpha * xc + mod_c[5] * y[:, :n_ctx], ln2_g[l], ln2_b[l])
            yl = y[:, n_ctx:]
        xl = _layer_norm(alpha * xl + mod_l[5] * yl, ln2_g[l], ln2_b[l])
    return xl
```

```python
import functools
import math

import jax
import jax.numpy as jnp
from jax import lax
from jax.experimental import pallas as pl
from jax.experimental.pallas import tpu as pltpu

D_MODEL = 2048
DEPTH = 4
GRID_W = 64
MIX_W = D_MODEL // 2
CHUNK = 64
GLA_HEADS = 4
GLA_DV = MIX_W // GLA_HEADS
GLA_DK = GLA_DV // 2
GLA_LR = 16
GLA_GATE_NORM = 16.0
RWKV_HEAD = 64
RWKV_HEADS = MIX_W // RWKV_HEAD
DECAY_LORA = 64
ICLR_LORA = 64
GATE_LORA = 128
RWKV_GN_EPS = 64e-5
GDN_HEAD = 128
GDN_HEADS = MIX_W // GDN_HEAD
CONV_K = 3
N_EXPERTS = 16
N_GROUPS = 4
TOP_K = 2
D_EXPERT = 1408
DISPATCH_BLOCK = 256

GLA_COLS = (GLA_HEADS * GLA_DK, GLA_HEADS * GLA_DK, MIX_W, MIX_W, GLA_LR, GLA_LR)
RWKV_COLS = (MIX_W, MIX_W, MIX_W, DECAY_LORA, DECAY_LORA, ICLR_LORA, ICLR_LORA, GATE_LORA)
GDN_COLS = (3 * MIX_W, MIX_W, GDN_HEADS, GDN_HEADS, GDN_HEADS, GDN_HEADS)
GLA_PROJ = sum(GLA_COLS)
RWKV_PROJ = sum(RWKV_COLS)
GDN_PROJ = sum(GDN_COLS)

LANE = 128
VMEM_LIMIT = 56 * 1024 * 1024
HIGHEST = lax.Precision.HIGHEST
NEG_BIG = -1e30


def _bf(x):
    return x.astype(jnp.bfloat16)


def _mm(a, b):
    return jnp.matmul(a, b)


def _split_cols(p, widths):
    out, s = [], 0
    for w in widths:
        out.append(p[..., s:s + w])
        s += w
    return out


def _layer_norm(x, g, b, eps=1e-5):
    xf = x.astype(jnp.float32)
    mu = xf.mean(-1, keepdims=True)
    var = jnp.square(xf - mu).mean(-1, keepdims=True)
    return ((xf - mu) * lax.rsqrt(var + eps)).astype(x.dtype) * g + b


def _rms_norm(x, w, eps=1e-6):
    xf = x.astype(jnp.float32)
    return (xf * lax.rsqrt(jnp.square(xf).mean(-1, keepdims=True) + eps)).astype(x.dtype) * w


def _l2_normalize(x, eps=1e-6):
    xf = x.astype(jnp.float32)
    return (xf * lax.rsqrt(jnp.square(xf).sum(-1, keepdims=True) + eps)).astype(x.dtype)


def _token_shift(p):
    pad = jnp.pad(p, ((0, 0), (1, 1), (0, 0)))
    return 0.5 * (pad[:, :-2] + pad[:, 2:])


def _grid_conv(t, w):
    b, n, ch = t.shape
    rows = n // GRID_W
    y = lax.conv_general_dilated(t.reshape(b, rows, GRID_W, ch), w[:, :, None, :], (1, 1), 'SAME',
                                 dimension_numbers=('NHWC', 'HWIO', 'NHWC'), feature_group_count=ch)
    return y.reshape(b, n, ch)


def _seq_conv(t, w):
    return lax.conv_general_dilated(t, w[:, None, :], (1,), 'SAME',
                                    dimension_numbers=('NWC', 'WIO', 'NWC'), feature_group_count=t.shape[-1])


def _bmm(a, b):
    return jnp.einsum('hmk,hkn->hmn', _bf(a), _bf(b), preferred_element_type=jnp.float32)


def _bmm_nt(a, b):
    return jnp.einsum('hmk,hnk->hmn', _bf(a), _bf(b), preferred_element_type=jnp.float32)


def _bmm_tn(a, b):
    return jnp.einsum('hkm,hkn->hmn', _bf(a), _bf(b), preferred_element_type=jnp.float32)


def _chunk_masks(reverse):
    c = CHUNK
    row = lax.broadcasted_iota(jnp.int32, (c, c), 0)
    col = lax.broadcasted_iota(jnp.int32, (c, c), 1)
    if reverse:
        return row <= col, row < col, c // 2, 0
    return row >= col, row > col, c // 2 - 1, c - 1


def _running_sum(x, incl):
    h = x.shape[0]
    hi = _bf(x)
    rest = x - hi.astype(jnp.float32)
    mid = _bf(rest)
    lo = _bf(rest - mid.astype(jnp.float32))
    tri = jnp.broadcast_to(jnp.where(incl, 1.0, 0.0).astype(jnp.bfloat16), (3 * h, CHUNK, CHUNK))
    s = jnp.einsum('hts,hsn->htn', tri, jnp.concatenate([hi, mid, lo], axis=0),
                   preferred_element_type=jnp.float32)
    return s[:h] + s[h:2 * h] + s[2 * h:]


def _unit_triangular_inverse(a_strict):
    c = a_strict.shape[-1]
    eye = (lax.broadcasted_iota(jnp.int32, (c, c), 0) == lax.broadcasted_iota(jnp.int32, (c, c), 1))
    q = -a_strict
    m = jnp.where(eye, 1.0, 0.0) + q
    q = _bmm(q, q)
    for _ in range(int(math.log2(c)) - 2):
        mq = _bmm(q, jnp.concatenate([m, q], axis=-1))
        m = m + mq[..., :c]
        q = mq[..., c:]
    return m + _bmm(q, m)


def _column_of(row_vec, n):
    return jnp.swapaxes(jnp.broadcast_to(row_vec, (row_vec.shape[0], n, n)), 1, 2)


def _rwkv_chain(shared, per_dir, t0, reverse):
    r, v, kk = shared
    lw, kd, b = per_dir
    c = CHUNK
    incl, strict, i_ref, i_last = _chunk_masks(reverse)
    cum = _running_sum(lw, incl)
    ref = cum[:, i_ref:i_ref + 1, :]
    last = cum[:, i_last:i_last + 1, :]
    rel = cum - ref
    e_inv = jnp.exp(-rel)
    p_ref = jnp.exp(ref)
    p_end = jnp.exp(last - ref)
    x = jnp.concatenate([kk * jnp.exp(rel - lw), r * jnp.exp(rel)], axis=1)
    y = jnp.concatenate([kd * e_inv, b * e_inv], axis=1)
    g = _bmm_nt(x, y)
    g_k = jnp.where(jnp.concatenate([strict, incl], axis=0), g[:, :, :c], 0.0)
    a_b = jnp.where(strict, g[:, :c, c:], 0.0)
    r_b = jnp.where(incl, g[:, c:, c:], 0.0)
    m = _unit_triangular_inverse(a_b)
    s = _bmm(x * p_ref, t0) + _bmm(g_k, v)
    u = _bmm(m, s[:, :c])
    out = s[:, c:] - _bmm(r_b, u)
    t1 = _column_of(jnp.exp(last), r.shape[2]) * t0 + _bmm_tn(y * p_end, jnp.concatenate([v, -u], axis=1))
    return out, t1


def _gla_chain(shared, per_dir, s0, reverse):
    q, k, v = shared
    (g,) = per_dir
    incl, _, i_ref, i_last = _chunk_masks(reverse)
    cum = _running_sum(g, incl)
    ref = cum[:, i_ref:i_ref + 1, :]
    last = cum[:, i_last:i_last + 1, :]
    att = jnp.where(incl, _bmm_nt(q * jnp.exp(cum - ref), k * jnp.exp(ref - cum)), 0.0)
    o = _bmm(att, v) + _bmm(q * jnp.exp(cum), s0)
    dk, dv = s0.shape[1], s0.shape[2]
    dec = _column_of(jnp.exp(last), dk)
    s1 = jnp.concatenate([dec] * (dv // dk), axis=-1) * s0 + _bmm_tn(k * jnp.exp(last - cum), v)
    return o, s1


def _gdn_chain(shared, per_dir, s0, reverse):
    q, k, v = shared
    la, beta = per_dir
    c = CHUNK
    incl, strict, _, i_last = _chunk_masks(reverse)
    gam = _running_sum(la, incl)
    last = gam[:, i_last:i_last + 1, :]
    gcol = gam[:, :, :c]
    decay = jnp.exp(jnp.where(incl, gcol - jnp.swapaxes(gcol, 1, 2), NEG_BIG))
    e_gam = jnp.exp(gam)
    a_mat = jnp.where(strict, beta[:, :, :c] * _bmm_nt(k, k) * decay, 0.0)
    rhs = jnp.concatenate([v * beta, k * (beta * e_gam)], axis=-1)
    sol = _bmm(_unit_triangular_inverse(a_mat), rhs)
    dv = v.shape[-1]
    u, w = sol[..., :dv], sol[..., dv:]
    qk = _bmm_nt(q, k) * decay
    delta = u - _bmm(w, s0)
    o = _bmm(q * e_gam, s0) + _bmm(qk, delta)
    s1 = jnp.exp(last) * s0 + _bmm_tn(k * jnp.exp(last - gam), delta)
    return o, s1


def _scan_kernel(*refs, chain, n_shared, n_dir):
    n_in = n_shared + n_dir
    f_in, b_in = refs[:n_in], refs[n_in:2 * n_in]
    yf, yb, sf, sb = refs[2 * n_in:]

    @pl.when(pl.program_id(1) == 0)
    def _():
        sf[...] = jnp.zeros_like(sf)
        sb[...] = jnp.zeros_like(sb)

    for ins, y, s, reverse in ((f_in, yf, sf, False), (b_in, yb, sb, True)):
        vals = [t[...] for t in ins]
        out, s1 = chain(vals[:n_shared], vals[n_shared:], s[...], reverse)
        y[...] = out
        s[...] = s1


def _bidir_scan(chain, shared, fwd, bwd, out_dim, state_dims, head_group, n_ctx, name):
    h, n, _ = shared[0].shape
    n_chunks = n // CHUNK
    ctx_chunks = n_ctx // CHUNK

    def fwd_map(g, i):
        return (g, i, 0)

    def bwd_map(g, i):
        return (g, jnp.where(i < ctx_chunks, ctx_chunks - 1 - i, n_chunks - 1 + ctx_chunks - i), 0)

    def specs(arrays, index_map):
        return [pl.BlockSpec((head_group, CHUNK, a.shape[2]), index_map) for a in arrays]

    ins = list(shared) + list(fwd)
    out = jax.ShapeDtypeStruct((h, n, out_dim), jnp.float32)
    state = pltpu.VMEM((head_group,) + state_dims, jnp.float32)
    return pl.pallas_call(
        functools.partial(_scan_kernel, chain=chain, n_shared=len(shared), n_dir=len(fwd)),
        grid=(h // head_group, n_chunks),
        in_specs=specs(ins, fwd_map) + specs(ins, bwd_map),
        out_specs=[pl.BlockSpec((head_group, CHUNK, out_dim), fwd_map),
                   pl.BlockSpec((head_group, CHUNK, out_dim), bwd_map)],
        out_shape=[out, out],
        scratch_shapes=[state, state],
        compiler_params=pltpu.CompilerParams(dimension_semantics=("arbitrary", "arbitrary"),
                                             vmem_limit_bytes=VMEM_LIMIT),
        name=name,
    )(*shared, *fwd, *shared, *bwd)


def _head_major(tc, tl):
    return jnp.transpose(jnp.concatenate([tc[0], tl[0]], axis=0), (1, 0, 2))


def _token_major(y, n_ctx):
    y = jnp.transpose(y, (1, 0, 2))[None]
    return y[:, :n_ctx], y[:, n_ctx:]


SEG_ROWS = 256
MM_TM = 1056
IN_TN = 896


def _modulate_kernel(x, scale, shift, o):
    o[...] = (x[...] * (1.0 + scale[0]) + shift[0]).astype(o.dtype)


def _modulate(x, scale, shift, n_ctx):
    n, d = x.shape
    assert n_ctx == SEG_ROWS and n % SEG_ROWS == 0
    seg = lambda i: (jnp.minimum(i, 1), 0, 0)
    return pl.pallas_call(
        _modulate_kernel,
        grid=(n // SEG_ROWS,),
        in_specs=[pl.BlockSpec((SEG_ROWS, d), lambda i: (i, 0)),
                  pl.BlockSpec((1, 1, d), seg), pl.BlockSpec((1, 1, d), seg)],
        out_specs=pl.BlockSpec((SEG_ROWS, d), lambda i: (i, 0)),
        out_shape=jax.ShapeDtypeStruct((n, d), jnp.bfloat16),
        compiler_params=pltpu.CompilerParams(dimension_semantics=("arbitrary",), vmem_limit_bytes=VMEM_LIMIT),
        name="modulate",
    )(x, scale[:, None, :], shift[:, None, :])


def _matmul_bias_kernel(x, w, b, o):
    o[...] = jnp.dot(x[...], w[...], preferred_element_type=jnp.float32) + b[...]


def _matmul_bias(x, w, b, tm, tn):
    m, k = x.shape
    n = w.shape[1]
    assert m % tm == 0 and n % tn == 0
    return pl.pallas_call(
        _matmul_bias_kernel,
        grid=(m // tm, n // tn),
        in_specs=[pl.BlockSpec((tm, k), lambda i, j: (i, 0)),
                  pl.BlockSpec((k, tn), lambda i, j: (0, j)),
                  pl.BlockSpec((1, tn), lambda i, j: (0, j))],
        out_specs=pl.BlockSpec((tm, tn), lambda i, j: (i, j)),
        out_shape=jax.ShapeDtypeStruct((m, n), jnp.float32),
        compiler_params=pltpu.CompilerParams(dimension_semantics=("arbitrary", "arbitrary"),
                                             vmem_limit_bytes=VMEM_LIMIT),
        name="matmul_bias",
    )(x, w, b)


_GLA_MAIN = GLA_PROJ - 2 * GLA_LR
_GDN_MAIN = GDN_PROJ - 4 * GDN_HEADS
_LR_PAD = LANE - 2 * GLA_LR
_AB_PAD = LANE - 4 * GDN_HEADS
OFF_GLA = 0
OFF_RWKV = _GLA_MAIN + LANE
OFF_GDN = OFF_RWKV + RWKV_PROJ
OFF_GATE = OFF_GDN + _GDN_MAIN + LANE
IN_PAD = OFF_GATE + 3 * D_MODEL


def _pad_in_cols(w):
    o_rwkv, o_gdn, o_gate = GLA_PROJ, GLA_PROJ + RWKV_PROJ, GLA_PROJ + RWKV_PROJ + GDN_PROJ
    z = lambda n: jnp.zeros(w.shape[:-1] + (n,), w.dtype)
    return jnp.concatenate([w[..., :o_rwkv], z(_LR_PAD), w[..., o_rwkv:o_gdn], w[..., o_gdn:o_gate], z(_AB_PAD),
                            w[..., o_gate:]], axis=-1)


def _gla_mixer(pc, pl_, dec_w2, dec_b, norm_w):
    n_ctx = pc.shape[1]

    def prep(p):
        bsz, n = p.shape[:2]
        q, k, v, og, lr_f, lr_b = _split_cols(p, GLA_COLS)
        heads = lambda t, d: t.reshape(bsz, n, GLA_HEADS, d)
        g = [heads(jax.nn.log_sigmoid((_mm(lr, dec_w2[d]) + dec_b[d]).astype(jnp.float32)) / GLA_GATE_NORM, GLA_DK)
             for d, lr in enumerate((lr_f, lr_b))]
        return heads(q, GLA_DK) * GLA_DK ** -0.5, heads(k, GLA_DK), heads(v, GLA_DV), og, g

    qc, kc, vc, ogc, gc = prep(pc)
    ql, kl, vl, ogl, gl = prep(pl_)
    of, ob = _bidir_scan(_gla_chain, (_head_major(qc, ql), _head_major(kc, kl), _head_major(vc, vl)),
                         (_head_major(gc[0], gl[0]),), (_head_major(gc[1], gl[1]),),
                         GLA_DV, (GLA_DK, GLA_DV), GLA_HEADS, n_ctx, "gla_scan")
    oc, ol = _token_major(of + ob, n_ctx)

    def finish(o, og):
        return _rms_norm(o, norm_w).reshape(og.shape) * jax.nn.silu(og)

    return finish(oc, ogc), finish(ol, ogl)


RWKV_HEAD_GROUP = 16


def _rwkv_mixer(pc, pl_, mu, w2, w0, a2, a0, g2, k_k, k_a, r_k, ln_w, ln_b):
    n_ctx = pc.shape[1]

    def prep(p):
        bsz, n = p.shape[:2]
        heads = lambda t: t.reshape(bsz, n, RWKV_HEADS, RWKV_HEAD)
        p = p + (_token_shift(p) - p) * mu
        r, k, v, wl_f, wl_b, al_f, al_b, gl = _split_cols(p, RWKV_COLS)
        kk = _l2_normalize(heads(k * k_k))
        per_dir = []
        for d, (wl, al) in enumerate(((wl_f, al_f), (wl_b, al_b))):
            z = w0[d] + _mm(jnp.tanh(wl), w2[d])
            log_decay = -jnp.exp(-jax.nn.softplus(-z) - 0.5)
            a = jax.nn.sigmoid(a0[d] + _mm(al, a2[d]))
            per_dir.append((heads(log_decay), heads(k * (1 + (a - 1) * k_a)), kk * heads(a)))
        g = _mm(jax.nn.sigmoid(gl), g2)
        return heads(r), heads(k), heads(v), kk, per_dir, g

    rc, kc, vc, kkc, dc, gc = prep(pc)
    rl, kl, vl, kkl, dl, gl = prep(pl_)
    yf, yb = _bidir_scan(_rwkv_chain, (_head_major(rc, rl), _head_major(vc, vl), _head_major(kkc, kkl)),
                         tuple(_head_major(tc, tl) for tc, tl in zip(dc[0], dl[0])),
                         tuple(_head_major(tc, tl) for tc, tl in zip(dc[1], dl[1])),
                         RWKV_HEAD, (RWKV_HEAD, RWKV_HEAD), RWKV_HEAD_GROUP, n_ctx, "rwkv7_scan")
    yc, yl = _token_major(yf + yb, n_ctx)

    def finish(y, r, k, v, g):
        yf = y.astype(jnp.float32)
        m = yf.mean(-1, keepdims=True)
        var = jnp.square(yf - m).mean(-1, keepdims=True)
        y = ((yf - m) * lax.rsqrt(var + RWKV_GN_EPS)).astype(y.dtype)
        y = y * ln_w.reshape(RWKV_HEADS, RWKV_HEAD) + ln_b.reshape(RWKV_HEADS, RWKV_HEAD)
        y = y + jnp.sum(r * k * r_k, axis=-1, keepdims=True) * v
        return y.reshape(g.shape) * g

    return finish(yc, rc, kc, vc, gc), finish(yl, rl, kl, vl, gl)


GDN_DEFAULT_PRECISION_WEIGHT = 0.75


def _to_chunks(t):
    b, n = t.shape[:2]
    t = t.reshape((b, n // CHUNK, CHUNK) + t.shape[2:])
    return jnp.swapaxes(t, 2, 3)


def _from_chunks(t):
    t = jnp.swapaxes(t, 2, 3)
    return t.reshape((t.shape[0], t.shape[1] * t.shape[2]) + t.shape[3:])


def _two_stage(scan_fn, ctx_args, lat_args, state0, reverse):
    if reverse:
        ctx_args = tuple(jnp.flip(t, axis=1) for t in ctx_args)
        lat_args = tuple(jnp.flip(t, axis=1) for t in lat_args)
    o_ctx, s_ctx = scan_fn(*ctx_args, state0)
    o_lat, _ = scan_fn(*lat_args, s_ctx)
    if reverse:
        o_ctx, o_lat = jnp.flip(o_ctx, axis=1), jnp.flip(o_lat, axis=1)
    return o_ctx, o_lat


def _gdn_chunked(q, k, v, log_a, beta, s0):
    dt = v.dtype
    f32 = jnp.float32
    q, k, v, log_a, beta = (_to_chunks(t.astype(f32)) for t in (q, k, v, log_a, beta))
    gam = jnp.cumsum(log_a, axis=-1)
    incl = jnp.tril(jnp.ones((CHUNK, CHUNK), bool))
    strict = jnp.tril(jnp.ones((CHUNK, CHUNK), bool), -1)
    decay = jnp.exp(jnp.where(incl, gam[..., :, None] - gam[..., None, :], -jnp.inf))
    a_mat = jnp.where(strict, beta[..., :, None] * jnp.einsum('bnhid,bnhjd->bnhij', k, k) * decay, 0.0)
    rhs = jnp.concatenate([v * beta[..., None], k * (beta * jnp.exp(gam))[..., None]], axis=-1)
    sol = lax.linalg.triangular_solve(a_mat, rhs, left_side=True, lower=True, unit_diagonal=True)
    dv = v.shape[-1]
    u, w = sol[..., :dv], sol[..., dv:]
    qk = jnp.einsum('bnhid,bnhjd->bnhij', q, k) * decay
    q_dec = q * jnp.exp(gam)[..., None]
    k_end = k * jnp.exp(gam[..., -1:] - gam)[..., None]
    dec = jnp.exp(gam[..., -1])

    def step(s, inp):
        u_n, w_n, qk_n, q_n, k_n, dec_n = inp
        delta = u_n - jnp.einsum('bhcd,bhdv->bhcv', w_n, s)
        o_n = jnp.einsum('bhcd,bhdv->bhcv', q_n, s) + jnp.einsum('bhij,bhjv->bhiv', qk_n, delta)
        s = dec_n[..., None, None] * s + jnp.einsum('bhcd,bhcv->bhdv', k_n, delta)
        return s, o_n

    xs = tuple(jnp.moveaxis(t, 1, 0) for t in (u, w, qk, q_dec, k_end, dec))
    s_fin, o = lax.scan(step, s0, xs)
    return _from_chunks(jnp.moveaxis(o, 0, 1)).astype(dt), s_fin


def _gdn_mixer(pc, pl_, conv_w, a_log, dt_bias, norm_w):
    n_ctx = pc.shape[1]

    def prep(p, on_grid):
        bsz, n = p.shape[:2]
        heads = lambda t: t.reshape(bsz, n, GDN_HEADS, GDN_HEAD)
        qkv, z, a_f, a_b, b_f, b_b = _split_cols(p, GDN_COLS)
        qkv = jax.nn.silu(_grid_conv(qkv, conv_w) if on_grid else _seq_conv(qkv, conv_w[CONV_K // 2]))
        q, k, v = jnp.split(qkv, 3, axis=-1)
        per_dir = [(-jnp.exp(a_log[d]) * jax.nn.softplus(a + dt_bias[d]), jax.nn.sigmoid(b))
                   for d, (a, b) in enumerate(((a_f, b_f), (a_b, b_b)))]
        return _l2_normalize(heads(q)) * GDN_HEAD ** -0.5, _l2_normalize(heads(k)), heads(v), z, per_dir

    qc, kc, vc, zc, dc = prep(pc, False)
    ql, kl, vl, zl, dl = prep(pl_, True)

    def lanes(tc, tl):
        t = jnp.transpose(jnp.concatenate([tc[0], tl[0]], axis=0), (1, 0))
        return jnp.broadcast_to(t[:, :, None], t.shape + (GDN_HEAD,))

    of, ob = _bidir_scan(_gdn_chain, (_head_major(qc, ql), _head_major(kc, kl), _head_major(vc, vl)),
                         tuple(lanes(tc, tl) for tc, tl in zip(dc[0], dl[0])),
                         tuple(lanes(tc, tl) for tc, tl in zip(dc[1], dl[1])),
                         GDN_HEAD, (GDN_HEAD, GDN_HEAD), GDN_HEADS, n_ctx, "gdn_scan")
    pc_o, pl_o = _token_major(of + ob, n_ctx)
    s0 = jnp.zeros((pc.shape[0], GDN_HEADS, GDN_HEAD, GDN_HEAD), jnp.float32)
    cf, lf = _two_stage(_gdn_chunked, (qc, kc, vc) + dc[0], (ql, kl, vl) + dl[0], s0, False)
    cb, lb = _two_stage(_gdn_chunked, (qc, kc, vc) + dc[1], (ql, kl, vl) + dl[1], s0, True)
    mix = lambda a, b: GDN_DEFAULT_PRECISION_WEIGHT * a + (1.0 - GDN_DEFAULT_PRECISION_WEIGHT) * b
    oc, ol = mix(cf + cb, pc_o), mix(lf + lb, pl_o)

    def finish(o, z):
        return _rms_norm(o, norm_w).reshape(z.shape) * jax.nn.silu(z)

    return finish(oc, zc), finish(ol, zl)


def _merge(gate_pre, outs, w_branch, w_out):
    gates = jax.nn.sigmoid(gate_pre)
    y = 0
    for m, o in enumerate(outs):
        y = y + gates[..., m * D_MODEL:(m + 1) * D_MODEL] * _mm(o, w_branch[m])
    return _mm(y, w_out)


def _expert_kernel(blk_e, n_used, x, w1, w3, w2, o):
    i = pl.program_id(0)

    @pl.when(i < n_used[0])
    def _():
        xe = x[...]
        h1 = jnp.dot(xe, w1[0], preferred_element_type=jnp.float32)
        h3 = jnp.dot(xe, w3[0], preferred_element_type=jnp.float32)
        o[...] = jnp.dot(_bf(jax.nn.silu(h1) * h3), w2[0], preferred_element_type=jnp.float32)

    @pl.when(i >= n_used[0])
    def _():
        o[...] = jnp.zeros_like(o)


def _expert_blocks(blk_e, n_used, xb, w1, w3, w2):
    n_slots, d = xb.shape
    de = w1.shape[2]
    n_blk = n_slots // DISPATCH_BLOCK
    grid_spec = pltpu.PrefetchScalarGridSpec(
        num_scalar_prefetch=2,
        grid=(n_blk,),
        in_specs=[pl.BlockSpec((DISPATCH_BLOCK, d), lambda i, be, nu: (i, 0)),
                  pl.BlockSpec((1, d, de), lambda i, be, nu: (be[i], 0, 0)),
                  pl.BlockSpec((1, d, de), lambda i, be, nu: (be[i], 0, 0)),
                  pl.BlockSpec((1, de, d), lambda i, be, nu: (be[i], 0, 0))],
        out_specs=pl.BlockSpec((DISPATCH_BLOCK, d), lambda i, be, nu: (i, 0)),
    )
    return pl.pallas_call(
        _expert_kernel,
        grid_spec=grid_spec,
        out_shape=jax.ShapeDtypeStruct((n_slots, d), jnp.float32),
        compiler_params=pltpu.CompilerParams(dimension_semantics=("arbitrary",), vmem_limit_bytes=VMEM_LIMIT),
        name="moe_experts",
    )(blk_e, n_used, xb, w1, w3, w2)


def _moe_ffn(h, router_w, router_bias, w1, w3, w2):
    n_tok, d = h.shape
    scores = jax.nn.sigmoid(_mm(h, router_w))
    sel = scores + router_bias.astype(jnp.float32)
    grp_top = lax.top_k(sel.reshape(n_tok, N_GROUPS, N_EXPERTS // N_GROUPS), 2)[0]
    best_group = jnp.argmax(grp_top.sum(-1), axis=-1)
    in_group = (jnp.arange(N_EXPERTS) // (N_EXPERTS // N_GROUPS))[None, :] == best_group[:, None]
    _, top_e = lax.top_k(jnp.where(in_group, sel, -jnp.inf), TOP_K)
    top_s = jnp.take_along_axis(scores, top_e, axis=-1)
    top_w = (top_s / top_s.sum(-1, keepdims=True)).astype(h.dtype)
    n_asg = n_tok * TOP_K
    flat_e = top_e.reshape(-1)
    order = jnp.argsort(flat_e)
    sorted_e = flat_e[order]
    counts = jnp.bincount(flat_e, length=N_EXPERTS)
    padded = (counts + DISPATCH_BLOCK - 1) // DISPATCH_BLOCK * DISPATCH_BLOCK
    pad_end = jnp.cumsum(padded)
    first = jnp.cumsum(counts) - counts
    dest = (pad_end - padded)[sorted_e] + jnp.arange(n_asg) - first[sorted_e]
    n_blk = -(-n_asg // DISPATCH_BLOCK) + N_EXPERTS
    tok = jnp.repeat(jnp.arange(n_tok, dtype=jnp.int32), TOP_K)
    slot_tok = jnp.full((n_blk * DISPATCH_BLOCK,), n_tok, jnp.int32).at[dest].set(tok[order])
    blk_e = jnp.minimum(jnp.searchsorted(pad_end, jnp.arange(n_blk) * DISPATCH_BLOCK, side='right'), N_EXPERTS - 1)
    h_pad = jnp.concatenate([_bf(h), jnp.zeros((1, d), jnp.bfloat16)], axis=0)
    n_used = (pad_end[-1] // DISPATCH_BLOCK).astype(jnp.int32).reshape(1)
    yb = _expert_blocks(blk_e.astype(jnp.int32), n_used, h_pad[slot_tok], _bf(w1), _bf(w3), _bf(w2))
    slot_of = jnp.zeros((n_asg,), jnp.int32).at[order].set(dest.astype(jnp.int32)).reshape(n_tok, TOP_K)
    y = 0
    for j in range(TOP_K):
        y = y + yb[slot_of[:, j]] * top_w[:, j:j + 1]
    return y


def kernel(x, c, ctx, c_ctx, w_ada, b_ada, w_in, b_in,
           gla_dec_w, gla_dec_b, gla_norm_w,
           rwkv_mu, rwkv_w2, rwkv_w0, rwkv_a2, rwkv_a0, rwkv_g2, rwkv_kk, rwkv_ka, rwkv_rk,
           rwkv_ln_w, rwkv_ln_b,
           gdn_conv_w, gdn_a_log, gdn_dt_bias, gdn_norm_w,
           w_branch, w_out, ln1_g, ln1_b,
           router_w, router_bias, moe_w1, moe_w3, moe_w2, ln2_g, ln2_b):
    alpha = (2.0 * DEPTH) ** 0.25
    bsz, n_lat, d = x.shape
    n_ctx = ctx.shape[1]
    silu_c = jax.nn.silu(c)[:, None, :]
    silu_cc = jax.nn.silu(c_ctx)
    xl, xc = x, ctx
    for l in range(DEPTH):
        last = l == DEPTH - 1
        mod_l = jnp.split(_mm(silu_c, w_ada[l]) + b_ada[l], 6, axis=-1)
        mod_c = jnp.split(_mm(silu_cc, w_ada[l]) + b_ada[l], 6, axis=-1)
        h = _modulate(jnp.concatenate([xc[0], xl[0]], axis=0),
                      jnp.stack([mod_c[1], mod_l[1][0, 0]]), jnp.stack([mod_c[0], mod_l[0][0, 0]]), n_ctx)
        p = _matmul_bias(h, _bf(_pad_in_cols(w_in[l])), _pad_in_cols(b_in[l])[None], MM_TM, IN_TN)
        slab = lambda off, width: (p[None, :n_ctx, off:off + width], p[None, n_ctx:, off:off + width])
        gla_c, gla_l = slab(OFF_GLA, GLA_PROJ)
        rwkv_c, rwkv_l = slab(OFF_RWKV, RWKV_PROJ)
        gdn_c, gdn_l = slab(OFF_GDN, GDN_PROJ)
        gate_c, gate_l = slab(OFF_GATE, 3 * D_MODEL)
        oa_c, oa_l = _gla_mixer(gla_c, gla_l, gla_dec_w[l], gla_dec_b[l], gla_norm_w[l])
        ob_c, ob_l = _rwkv_mixer(rwkv_c, rwkv_l, rwkv_mu[l], rwkv_w2[l], rwkv_w0[l], rwkv_a2[l], rwkv_a0[l],
                                 rwkv_g2[l], rwkv_kk[l], rwkv_ka[l], rwkv_rk[l], rwkv_ln_w[l], rwkv_ln_b[l])
        oc_c, oc_l = _gdn_mixer(gdn_c, gdn_l, gdn_conv_w[l], gdn_a_log[l], gdn_dt_bias[l], gdn_norm_w[l])
        mix_l = _merge(gate_l, (oa_l, ob_l, oc_l), w_branch[l], w_out[l])
        xl = _layer_norm(alpha * xl + mod_l[2] * mix_l, ln1_g[l], ln1_b[l])
        h2l = xl * (1 + mod_l[4]) + mod_l[3]
        if last:
            yl = _moe_ffn(h2l.reshape(-1, d), router_w, router_bias,
                          moe_w1[l], moe_w3[l], moe_w2[l]).reshape(xl.shape)
        else:
            mix_c = _merge(gate_c, (oa_c, ob_c, oc_c), w_branch[l], w_out[l])
            xc = _layer_norm(alpha * xc + mod_c[2] * mix_c, ln1_g[l], ln1_b[l])
            h2c = xc * (1 + mod_c[4]) + mod_c[3]
            y = _moe_ffn(jnp.concatenate([h2c, h2l], axis=1).reshape(-1, d), router_w, router_bias,
                         moe_w1[l], moe_w3[l], moe_w2[l]).reshape(bsz, n_ctx + n_lat, d)
            xc = _layer_norm(alpha * xc + mod_c[5] * y[:, :n_ctx], ln2_g[l], ln2_b[l])
            yl = y[:, n_ctx:]
        xl = _layer_norm(alpha * xl + mod_l[5] * yl, ln2_g[l], ln2_b[l])
    return xl
```

```python
import functools
import math

import jax
import jax.numpy as jnp
from jax import lax
from jax.experimental import pallas as pl
from jax.experimental.pallas import tpu as pltpu

D_MODEL = 2048
DEPTH = 4
GRID_W = 64
MIX_W = D_MODEL // 2
CHUNK = 64
GLA_HEADS = 4
GLA_DV = MIX_W // GLA_HEADS
GLA_DK = GLA_DV // 2
GLA_LR = 16
GLA_GATE_NORM = 16.0
RWKV_HEAD = 64
RWKV_HEADS = MIX_W // RWKV_HEAD
DECAY_LORA = 64
ICLR_LORA = 64
GATE_LORA = 128
RWKV_GN_EPS = 64e-5
GDN_HEAD = 128
GDN_HEADS = MIX_W // GDN_HEAD
CONV_K = 3
N_EXPERTS = 16
N_GROUPS = 4
TOP_K = 2
D_EXPERT = 1408
DISPATCH_BLOCK = 256

GLA_COLS = (GLA_HEADS * GLA_DK, GLA_HEADS * GLA_DK, MIX_W, MIX_W, GLA_LR, GLA_LR)
RWKV_COLS = (MIX_W, MIX_W, MIX_W, DECAY_LORA, DECAY_LORA, ICLR_LORA, ICLR_LORA, GATE_LORA)
GDN_COLS = (3 * MIX_W, MIX_W, GDN_HEADS, GDN_HEADS, GDN_HEADS, GDN_HEADS)
GLA_PROJ = sum(GLA_COLS)
RWKV_PROJ = sum(RWKV_COLS)
GDN_PROJ = sum(GDN_COLS)

LANE = 128
VMEM_LIMIT = 56 * 1024 * 1024
HIGHEST = lax.Precision.HIGHEST
NEG_BIG = -1e30


def _bf(x):
    return x.astype(jnp.bfloat16)


def _mm(a, b):
    return jnp.matmul(a, b)


def _split_cols(p, widths):
    out, s = [], 0
    for w in widths:
        out.append(p[..., s:s + w])
        s += w
    return out


def _layer_norm(x, g, b, eps=1e-5):
    xf = x.astype(jnp.float32)
    mu = xf.mean(-1, keepdims=True)
    var = jnp.square(xf - mu).mean(-1, keepdims=True)
    return ((xf - mu) * lax.rsqrt(var + eps)).astype(x.dtype) * g + b


def _rms_norm(x, w, eps=1e-6):
    xf = x.astype(jnp.float32)
    return (xf * lax.rsqrt(jnp.square(xf).mean(-1, keepdims=True) + eps)).astype(x.dtype) * w


def _l2_normalize(x, eps=1e-6):
    xf = x.astype(jnp.float32)
    return (xf * lax.rsqrt(jnp.square(xf).sum(-1, keepdims=True) + eps)).astype(x.dtype)


def _token_shift(p):
    pad = jnp.pad(p, ((0, 0), (1, 1), (0, 0)))
    return 0.5 * (pad[:, :-2] + pad[:, 2:])


def _grid_conv(t, w):
    b, n, ch = t.shape
    rows = n // GRID_W
    y = lax.conv_general_dilated(t.reshape(b, rows, GRID_W, ch), w[:, :, None, :], (1, 1), 'SAME',
                                 dimension_numbers=('NHWC', 'HWIO', 'NHWC'), feature_group_count=ch)
    return y.reshape(b, n, ch)


def _seq_conv(t, w):
    return lax.conv_general_dilated(t, w[:, None, :], (1,), 'SAME',
                                    dimension_numbers=('NWC', 'WIO', 'NWC'), feature_group_count=t.shape[-1])


def _bmm(a, b):
    return jnp.einsum('hmk,hkn->hmn', _bf(a), _bf(b), preferred_element_type=jnp.float32)


def _bmm_nt(a, b):
    return jnp.einsum('hmk,hnk->hmn', _bf(a), _bf(b), preferred_element_type=jnp.float32)


def _bmm_tn(a, b):
    return jnp.einsum('hkm,hkn->hmn', _bf(a), _bf(b), preferred_element_type=jnp.float32)


def _chunk_masks(reverse):
    c = CHUNK
    row = lax.broadcasted_iota(jnp.int32, (c, c), 0)
    col = lax.broadcasted_iota(jnp.int32, (c, c), 1)
    if reverse:
        return row <= col, row < col, c // 2, 0
    return row >= col, row > col, c // 2 - 1, c - 1


def _running_sum(x, incl):
    h = x.shape[0]
    hi = _bf(x)
    rest = x - hi.astype(jnp.float32)
    mid = _bf(rest)
    lo = _bf(rest - mid.astype(jnp.float32))
    tri = jnp.broadcast_to(jnp.where(incl, 1.0, 0.0).astype(jnp.bfloat16), (3 * h, CHUNK, CHUNK))
    s = jnp.einsum('hts,hsn->htn', tri, jnp.concatenate([hi, mid, lo], axis=0),
                   preferred_element_type=jnp.float32)
    return s[:h] + s[h:2 * h] + s[2 * h:]


def _unit_triangular_inverse(a_strict):
    c = a_strict.shape[-1]
    eye = (lax.broadcasted_iota(jnp.int32, (c, c), 0) == lax.broadcasted_iota(jnp.int32, (c, c), 1))
    q = -a_strict
    m = jnp.where(eye, 1.0, 0.0) + q
    q = _bmm(q, q)
    for _ in range(int(math.log2(c)) - 2):
        mq = _bmm(q, jnp.concatenate([m, q], axis=-1))
        m = m + mq[..., :c]
        q = mq[..., c:]
    return m + _bmm(q, m)


def _column_of(row_vec, n):
    return jnp.swapaxes(jnp.broadcast_to(row_vec, (row_vec.shape[0], n, n)), 1, 2)


def _rwkv_chain(shared, per_dir, t0, reverse):
    r, v, kk = shared
    lw, kd, b = per_dir
    c = CHUNK
    incl, strict, i_ref, i_last = _chunk_masks(reverse)
    cum = _running_sum(lw, incl)
    ref = cum[:, i_ref:i_ref + 1, :]
    last = cum[:, i_last:i_last + 1, :]
    rel = cum - ref
    e_inv = jnp.exp(-rel)
    p_ref = jnp.exp(ref)
    p_end = jnp.exp(last - ref)
    x = jnp.concatenate([kk * jnp.exp(rel - lw), r * jnp.exp(rel)], axis=1)
    y = jnp.concatenate([kd * e_inv, b * e_inv], axis=1)
    g = _bmm_nt(x, y)
    g_k = jnp.where(jnp.concatenate([strict, incl], axis=0), g[:, :, :c], 0.0)
    a_b = jnp.where(strict, g[:, :c, c:], 0.0)
    r_b = jnp.where(incl, g[:, c:, c:], 0.0)
    m = _unit_triangular_inverse(a_b)
    s = _bmm(x * p_ref, t0) + _bmm(g_k, v)
    u = _bmm(m, s[:, :c])
    out = s[:, c:] - _bmm(r_b, u)
    t1 = _column_of(jnp.exp(last), r.shape[2]) * t0 + _bmm_tn(y * p_end, jnp.concatenate([v, -u], axis=1))
    return out, t1


def _gla_chain(shared, per_dir, s0, reverse):
    q, k, v = shared
    (g,) = per_dir
    incl, _, i_ref, i_last = _chunk_masks(reverse)
    cum = _running_sum(g, incl)
    ref = cum[:, i_ref:i_ref + 1, :]
    last = cum[:, i_last:i_last + 1, :]
    att = jnp.where(incl, _bmm_nt(q * jnp.exp(cum - ref), k * jnp.exp(ref - cum)), 0.0)
    o = _bmm(att, v) + _bmm(q * jnp.exp(cum), s0)
    dk, dv = s0.shape[1], s0.shape[2]
    dec = _column_of(jnp.exp(last), dk)
    s1 = jnp.concatenate([dec] * (dv // dk), axis=-1) * s0 + _bmm_tn(k * jnp.exp(last - cum), v)
    return o, s1


def _gdn_chain(shared, per_dir, s0, reverse):
    q, k, v = shared
    la, beta = per_dir
    c = CHUNK
    incl, strict, _, i_last = _chunk_masks(reverse)
    gam = _running_sum(la, incl)
    last = gam[:, i_last:i_last + 1, :]
    gcol = gam[:, :, :c]
    decay = jnp.exp(jnp.where(incl, gcol - jnp.swapaxes(gcol, 1, 2), NEG_BIG))
    e_gam = jnp.exp(gam)
    a_mat = jnp.where(strict, beta[:, :, :c] * _bmm_nt(k, k) * decay, 0.0)
    rhs = jnp.concatenate([v * beta, k * (beta * e_gam)], axis=-1)
    sol = _bmm(_unit_triangular_inverse(a_mat), rhs)
    dv = v.shape[-1]
    u, w = sol[..., :dv], sol[..., dv:]
    qk = _bmm_nt(q, k) * decay
    delta = u - _bmm(w, s0)
    o = _bmm(q * e_gam, s0) + _bmm(qk, delta)
    s1 = jnp.exp(last) * s0 + _bmm_tn(k * jnp.exp(last - gam), delta)
    return o, s1


def _scan_kernel(*refs, chain, n_shared, n_dir):
    n_in = n_shared + n_dir
    f_in, b_in = refs[:n_in], refs[n_in:2 * n_in]
    yf, yb, sf, sb = refs[2 * n_in:]

    @pl.when(pl.program_id(1) == 0)
    def _():
        sf[...] = jnp.zeros_like(sf)
        sb[...] = jnp.zeros_like(sb)

    for ins, y, s, reverse in ((f_in, yf, sf, False), (b_in, yb, sb, True)):
        vals = [t[...] for t in ins]
        out, s1 = chain(vals[:n_shared], vals[n_shared:], s[...], reverse)
        y[...] = out
        s[...] = s1


def _bidir_scan(chain, shared, fwd, bwd, out_dim, state_dims, head_group, n_ctx, name):
    h, n, _ = shared[0].shape
    n_chunks = n // CHUNK
    ctx_chunks = n_ctx // CHUNK

    def fwd_map(g, i):
        return (g, i, 0)

    def bwd_map(g, i):
        return (g, jnp.where(i < ctx_chunks, ctx_chunks - 1 - i, n_chunks - 1 + ctx_chunks - i), 0)

    def specs(arrays, index_map):
        return [pl.BlockSpec((head_group, CHUNK, a.shape[2]), index_map) for a in arrays]

    ins = list(shared) + list(fwd)
    out = jax.ShapeDtypeStruct((h, n, out_dim), jnp.float32)
    state = pltpu.VMEM((head_group,) + state_dims, jnp.float32)
    return pl.pallas_call(
        functools.partial(_scan_kernel, chain=chain, n_shared=len(shared), n_dir=len(fwd)),
        grid=(h // head_group, n_chunks),
        in_specs=specs(ins, fwd_map) + specs(ins, bwd_map),
        out_specs=[pl.BlockSpec((head_group, CHUNK, out_dim), fwd_map),
                   pl.BlockSpec((head_group, CHUNK, out_dim), bwd_map)],
        out_shape=[out, out],
        scratch_shapes=[state, state],
        compiler_params=pltpu.CompilerParams(dimension_semantics=("arbitrary", "arbitrary"),
                                             vmem_limit_bytes=VMEM_LIMIT),
        name=name,
    )(*shared, *fwd, *shared, *bwd)


def _head_major(tc, tl):
    return jnp.transpose(jnp.concatenate([tc[0], tl[0]], axis=0), (1, 0, 2))


def _token_major(y, n_ctx):
    y = jnp.transpose(y, (1, 0, 2))[None]
    return y[:, :n_ctx], y[:, n_ctx:]


SEG_ROWS = 256
MM_TM = 1056
IN_TN = 896


def _modulate_kernel(x, scale, shift, o):
    o[...] = (x[...] * (1.0 + scale[0]) + shift[0]).astype(o.dtype)


def _modulate(x, scale, shift, n_ctx):
    n, d = x.shape
    assert n_ctx == SEG_ROWS and n % SEG_ROWS == 0
    seg = lambda i: (jnp.minimum(i, 1), 0, 0)
    return pl.pallas_call(
        _modulate_kernel,
        grid=(n // SEG_ROWS,),
        in_specs=[pl.BlockSpec((SEG_ROWS, d), lambda i: (i, 0)),
                  pl.BlockSpec((1, 1, d), seg), pl.BlockSpec((1, 1, d), seg)],
        out_specs=pl.BlockSpec((SEG_ROWS, d), lambda i: (i, 0)),
        out_shape=jax.ShapeDtypeStruct((n, d), jnp.bfloat16),
        compiler_params=pltpu.CompilerParams(dimension_semantics=("arbitrary",), vmem_limit_bytes=VMEM_LIMIT),
        name="modulate",
    )(x, scale[:, None, :], shift[:, None, :])


def _matmul_bias_kernel(x, w, b, o):
    o[...] = jnp.dot(x[...], w[0], preferred_element_type=jnp.float32) + b[0]


def _matmul_bias(x, w, b, layer, tm, tn):
    m, k = x.shape
    n = w.shape[2]
    assert m % tm == 0 and n % tn == 0
    return pl.pallas_call(
        _matmul_bias_kernel,
        grid=(m // tm, n // tn),
        in_specs=[pl.BlockSpec((tm, k), lambda i, j: (i, 0)),
                  pl.BlockSpec((1, k, tn), lambda i, j: (layer, 0, j)),
                  pl.BlockSpec((1, 1, tn), lambda i, j: (layer, 0, j))],
        out_specs=pl.BlockSpec((tm, tn), lambda i, j: (i, j)),
        out_shape=jax.ShapeDtypeStruct((m, n), jnp.float32),
        compiler_params=pltpu.CompilerParams(dimension_semantics=("arbitrary", "arbitrary"),
                                             vmem_limit_bytes=VMEM_LIMIT),
        name="matmul_bias",
    )(x, w, b)


_GLA_MAIN = GLA_PROJ - 2 * GLA_LR
_GDN_MAIN = GDN_PROJ - 4 * GDN_HEADS
_LR_PAD = LANE - 2 * GLA_LR
_AB_PAD = LANE - 4 * GDN_HEADS
OFF_GLA = 0
OFF_RWKV = _GLA_MAIN + LANE
OFF_GDN = OFF_RWKV + RWKV_PROJ
OFF_GATE = OFF_GDN + _GDN_MAIN + LANE
IN_PAD = OFF_GATE + 3 * D_MODEL


def _pad_in_cols(w):
    o_rwkv, o_gdn, o_gate = GLA_PROJ, GLA_PROJ + RWKV_PROJ, GLA_PROJ + RWKV_PROJ + GDN_PROJ
    z = lambda n: jnp.zeros(w.shape[:-1] + (n,), w.dtype)
    return jnp.concatenate([w[..., :o_rwkv], z(_LR_PAD), w[..., o_rwkv:o_gdn], w[..., o_gdn:o_gate], z(_AB_PAD),
                            w[..., o_gate:]], axis=-1)


def _gla_mixer(pc, pl_, dec_w2, dec_b, norm_w):
    n_ctx = pc.shape[1]

    def prep(p):
        bsz, n = p.shape[:2]
        q, k, v, og, lr_f, lr_b = _split_cols(p, GLA_COLS)
        heads = lambda t, d: t.reshape(bsz, n, GLA_HEADS, d)
        g = [heads(jax.nn.log_sigmoid((_mm(lr, dec_w2[d]) + dec_b[d]).astype(jnp.float32)) / GLA_GATE_NORM, GLA_DK)
             for d, lr in enumerate((lr_f, lr_b))]
        return heads(q, GLA_DK) * GLA_DK ** -0.5, heads(k, GLA_DK), heads(v, GLA_DV), og, g

    qc, kc, vc, ogc, gc = prep(pc)
    ql, kl, vl, ogl, gl = prep(pl_)
    of, ob = _bidir_scan(_gla_chain, (_head_major(qc, ql), _head_major(kc, kl), _head_major(vc, vl)),
                         (_head_major(gc[0], gl[0]),), (_head_major(gc[1], gl[1]),),
                         GLA_DV, (GLA_DK, GLA_DV), GLA_HEADS, n_ctx, "gla_scan")
    oc, ol = _token_major(of + ob, n_ctx)

    def finish(o, og):
        return _rms_norm(o, norm_w).reshape(og.shape) * jax.nn.silu(og)

    return finish(oc, ogc), finish(ol, ogl)


RWKV_HEAD_GROUP = 16


def _rwkv_mixer(pc, pl_, mu, w2, w0, a2, a0, g2, k_k, k_a, r_k, ln_w, ln_b):
    n_ctx = pc.shape[1]

    def prep(p):
        bsz, n = p.shape[:2]
        heads = lambda t: t.reshape(bsz, n, RWKV_HEADS, RWKV_HEAD)
        p = p + (_token_shift(p) - p) * mu
        r, k, v, wl_f, wl_b, al_f, al_b, gl = _split_cols(p, RWKV_COLS)
        kk = _l2_normalize(heads(k * k_k))
        per_dir = []
        for d, (wl, al) in enumerate(((wl_f, al_f), (wl_b, al_b))):
            z = w0[d] + _mm(jnp.tanh(wl), w2[d])
            log_decay = -jnp.exp(-jax.nn.softplus(-z) - 0.5)
            a = jax.nn.sigmoid(a0[d] + _mm(al, a2[d]))
            per_dir.append((heads(log_decay), heads(k * (1 + (a - 1) * k_a)), kk * heads(a)))
        g = _mm(jax.nn.sigmoid(gl), g2)
        return heads(r), heads(k), heads(v), kk, per_dir, g

    rc, kc, vc, kkc, dc, gc = prep(pc)
    rl, kl, vl, kkl, dl, gl = prep(pl_)
    yf, yb = _bidir_scan(_rwkv_chain, (_head_major(rc, rl), _head_major(vc, vl), _head_major(kkc, kkl)),
                         tuple(_head_major(tc, tl) for tc, tl in zip(dc[0], dl[0])),
                         tuple(_head_major(tc, tl) for tc, tl in zip(dc[1], dl[1])),
                         RWKV_HEAD, (RWKV_HEAD, RWKV_HEAD), RWKV_HEAD_GROUP, n_ctx, "rwkv7_scan")
    yc, yl = _token_major(yf + yb, n_ctx)

    def finish(y, r, k, v, g):
        yf = y.astype(jnp.float32)
        m = yf.mean(-1, keepdims=True)
        var = jnp.square(yf - m).mean(-1, keepdims=True)
        y = ((yf - m) * lax.rsqrt(var + RWKV_GN_EPS)).astype(y.dtype)
        y = y * ln_w.reshape(RWKV_HEADS, RWKV_HEAD) + ln_b.reshape(RWKV_HEADS, RWKV_HEAD)
        y = y + jnp.sum(r * k * r_k, axis=-1, keepdims=True) * v
        return y.reshape(g.shape) * g

    return finish(yc, rc, kc, vc, gc), finish(yl, rl, kl, vl, gl)


GDN_DEFAULT_PRECISION_WEIGHT = 0.75


def _to_chunks(t):
    b, n = t.shape[:2]
    t = t.reshape((b, n // CHUNK, CHUNK) + t.shape[2:])
    return jnp.swapaxes(t, 2, 3)


def _from_chunks(t):
    t = jnp.swapaxes(t, 2, 3)
    return t.reshape((t.shape[0], t.shape[1] * t.shape[2]) + t.shape[3:])


def _two_stage(scan_fn, ctx_args, lat_args, state0, reverse):
    if reverse:
        ctx_args = tuple(jnp.flip(t, axis=1) for t in ctx_args)
        lat_args = tuple(jnp.flip(t, axis=1) for t in lat_args)
    o_ctx, s_ctx = scan_fn(*ctx_args, state0)
    o_lat, _ = scan_fn(*lat_args, s_ctx)
    if reverse:
        o_ctx, o_lat = jnp.flip(o_ctx, axis=1), jnp.flip(o_lat, axis=1)
    return o_ctx, o_lat


def _gdn_chunked(q, k, v, log_a, beta, s0):
    dt = v.dtype
    f32 = jnp.float32
    q, k, v, log_a, beta = (_to_chunks(t.astype(f32)) for t in (q, k, v, log_a, beta))
    gam = jnp.cumsum(log_a, axis=-1)
    incl = jnp.tril(jnp.ones((CHUNK, CHUNK), bool))
    strict = jnp.tril(jnp.ones((CHUNK, CHUNK), bool), -1)
    decay = jnp.exp(jnp.where(incl, gam[..., :, None] - gam[..., None, :], -jnp.inf))
    a_mat = jnp.where(strict, beta[..., :, None] * jnp.einsum('bnhid,bnhjd->bnhij', k, k) * decay, 0.0)
    rhs = jnp.concatenate([v * beta[..., None], k * (beta * jnp.exp(gam))[..., None]], axis=-1)
    sol = lax.linalg.triangular_solve(a_mat, rhs, left_side=True, lower=True, unit_diagonal=True)
    dv = v.shape[-1]
    u, w = sol[..., :dv], sol[..., dv:]
    qk = jnp.einsum('bnhid,bnhjd->bnhij', q, k) * decay
    q_dec = q * jnp.exp(gam)[..., None]
    k_end = k * jnp.exp(gam[..., -1:] - gam)[..., None]
    dec = jnp.exp(gam[..., -1])

    def step(s, inp):
        u_n, w_n, qk_n, q_n, k_n, dec_n = inp
        delta = u_n - jnp.einsum('bhcd,bhdv->bhcv', w_n, s)
        o_n = jnp.einsum('bhcd,bhdv->bhcv', q_n, s) + jnp.einsum('bhij,bhjv->bhiv', qk_n, delta)
        s = dec_n[..., None, None] * s + jnp.einsum('bhcd,bhcv->bhdv', k_n, delta)
        return s, o_n

    xs = tuple(jnp.moveaxis(t, 1, 0) for t in (u, w, qk, q_dec, k_end, dec))
    s_fin, o = lax.scan(step, s0, xs)
    return _from_chunks(jnp.moveaxis(o, 0, 1)).astype(dt), s_fin


def _gdn_mixer(pc, pl_, conv_w, a_log, dt_bias, norm_w):
    n_ctx = pc.shape[1]

    def prep(p, on_grid):
        bsz, n = p.shape[:2]
        heads = lambda t: t.reshape(bsz, n, GDN_HEADS, GDN_HEAD)
        qkv, z, a_f, a_b, b_f, b_b = _split_cols(p, GDN_COLS)
        qkv = jax.nn.silu(_grid_conv(qkv, conv_w) if on_grid else _seq_conv(qkv, conv_w[CONV_K // 2]))
        q, k, v = jnp.split(qkv, 3, axis=-1)
        per_dir = [(-jnp.exp(a_log[d]) * jax.nn.softplus(a + dt_bias[d]), jax.nn.sigmoid(b))
                   for d, (a, b) in enumerate(((a_f, b_f), (a_b, b_b)))]
        return _l2_normalize(heads(q)) * GDN_HEAD ** -0.5, _l2_normalize(heads(k)), heads(v), z, per_dir

    qc, kc, vc, zc, dc = prep(pc, False)
    ql, kl, vl, zl, dl = prep(pl_, True)

    def lanes(tc, tl):
        t = jnp.transpose(jnp.concatenate([tc[0], tl[0]], axis=0), (1, 0))
        return jnp.broadcast_to(t[:, :, None], t.shape + (GDN_HEAD,))

    of, ob = _bidir_scan(_gdn_chain, (_head_major(qc, ql), _head_major(kc, kl), _head_major(vc, vl)),
                         tuple(lanes(tc, tl) for tc, tl in zip(dc[0], dl[0])),
                         tuple(lanes(tc, tl) for tc, tl in zip(dc[1], dl[1])),
                         GDN_HEAD, (GDN_HEAD, GDN_HEAD), GDN_HEADS, n_ctx, "gdn_scan")
    pc_o, pl_o = _token_major(of + ob, n_ctx)
    s0 = jnp.zeros((pc.shape[0], GDN_HEADS, GDN_HEAD, GDN_HEAD), jnp.float32)
    cf, lf = _two_stage(_gdn_chunked, (qc, kc, vc) + dc[0], (ql, kl, vl) + dl[0], s0, False)
    cb, lb = _two_stage(_gdn_chunked, (qc, kc, vc) + dc[1], (ql, kl, vl) + dl[1], s0, True)
    mix = lambda a, b: GDN_DEFAULT_PRECISION_WEIGHT * a + (1.0 - GDN_DEFAULT_PRECISION_WEIGHT) * b
    oc, ol = mix(cf + cb, pc_o), mix(lf + lb, pl_o)

    def finish(o, z):
        return _rms_norm(o, norm_w).reshape(z.shape) * jax.nn.silu(z)

    return finish(oc, zc), finish(ol, zl)


def _merge(gate_pre, outs, w_branch, w_out):
    gates = jax.nn.sigmoid(gate_pre)
    y = 0
    for m, o in enumerate(outs):
        y = y + gates[..., m * D_MODEL:(m + 1) * D_MODEL] * _mm(o, w_branch[m])
    return _mm(y, w_out)


def _expert_kernel(blk_e, n_used, x, w1, w3, w2, o):
    i = pl.program_id(0)

    @pl.when(i < n_used[0])
    def _():
        xe = x[...]
        h1 = jnp.dot(xe, w1[0, 0], preferred_element_type=jnp.float32)
        h3 = jnp.dot(xe, w3[0, 0], preferred_element_type=jnp.float32)
        o[...] = jnp.dot(_bf(jax.nn.silu(h1) * h3), w2[0, 0], preferred_element_type=jnp.float32)

    @pl.when(i >= n_used[0])
    def _():
        o[...] = jnp.zeros_like(o)


def _expert_blocks(blk_e, n_used, xb, w1, w3, w2, layer):
    n_slots, d = xb.shape
    de = w1.shape[3]
    n_blk = n_slots // DISPATCH_BLOCK
    grid_spec = pltpu.PrefetchScalarGridSpec(
        num_scalar_prefetch=2,
        grid=(n_blk,),
        in_specs=[pl.BlockSpec((DISPATCH_BLOCK, d), lambda i, be, nu: (i, 0)),
                  pl.BlockSpec((1, 1, d, de), lambda i, be, nu: (layer, be[i], 0, 0)),
                  pl.BlockSpec((1, 1, d, de), lambda i, be, nu: (layer, be[i], 0, 0)),
                  pl.BlockSpec((1, 1, de, d), lambda i, be, nu: (layer, be[i], 0, 0))],
        out_specs=pl.BlockSpec((DISPATCH_BLOCK, d), lambda i, be, nu: (i, 0)),
    )
    return pl.pallas_call(
        _expert_kernel,
        grid_spec=grid_spec,
        out_shape=jax.ShapeDtypeStruct((n_slots, d), jnp.float32),
        compiler_params=pltpu.CompilerParams(dimension_semantics=("arbitrary",), vmem_limit_bytes=VMEM_LIMIT),
        name="moe_experts",
    )(blk_e, n_used, xb, w1, w3, w2)


def _moe_ffn(h, router_w, router_bias, w1, w3, w2, layer):
    n_tok, d = h.shape
    scores = jax.nn.sigmoid(_mm(h, router_w))
    sel = scores + router_bias.astype(jnp.float32)
    grp_top = lax.top_k(sel.reshape(n_tok, N_GROUPS, N_EXPERTS // N_GROUPS), 2)[0]
    best_group = jnp.argmax(grp_top.sum(-1), axis=-1)
    in_group = (jnp.arange(N_EXPERTS) // (N_EXPERTS // N_GROUPS))[None, :] == best_group[:, None]
    _, top_e = lax.top_k(jnp.where(in_group, sel, -jnp.inf), TOP_K)
    top_s = jnp.take_along_axis(scores, top_e, axis=-1)
    top_w = (top_s / top_s.sum(-1, keepdims=True)).astype(h.dtype)
    n_asg = n_tok * TOP_K
    flat_e = top_e.reshape(-1)
    order = jnp.argsort(flat_e)
    sorted_e = flat_e[order]
    counts = jnp.bincount(flat_e, length=N_EXPERTS)
    padded = (counts + DISPATCH_BLOCK - 1) // DISPATCH_BLOCK * DISPATCH_BLOCK
    pad_end = jnp.cumsum(padded)
    first = jnp.cumsum(counts) - counts
    dest = (pad_end - padded)[sorted_e] + jnp.arange(n_asg) - first[sorted_e]
    n_blk = -(-n_asg // DISPATCH_BLOCK) + N_EXPERTS
    tok = jnp.repeat(jnp.arange(n_tok, dtype=jnp.int32), TOP_K)
    slot_tok = jnp.full((n_blk * DISPATCH_BLOCK,), n_tok, jnp.int32).at[dest].set(tok[order])
    blk_e = jnp.minimum(jnp.searchsorted(pad_end, jnp.arange(n_blk) * DISPATCH_BLOCK, side='right'), N_EXPERTS - 1)
    h_pad = jnp.concatenate([_bf(h), jnp.zeros((1, d), jnp.bfloat16)], axis=0)
    n_used = (pad_end[-1] // DISPATCH_BLOCK).astype(jnp.int32).reshape(1)
    yb = _expert_blocks(blk_e.astype(jnp.int32), n_used, h_pad[slot_tok], w1, w3, w2, layer)
    slot_of = jnp.zeros((n_asg,), jnp.int32).at[order].set(dest.astype(jnp.int32)).reshape(n_tok, TOP_K)
    y = 0
    for j in range(TOP_K):
        y = y + yb[slot_of[:, j]] * top_w[:, j:j + 1]
    return y


def kernel(x, c, ctx, c_ctx, w_ada, b_ada, w_in, b_in,
           gla_dec_w, gla_dec_b, gla_norm_w,
           rwkv_mu, rwkv_w2, rwkv_w0, rwkv_a2, rwkv_a0, rwkv_g2, rwkv_kk, rwkv_ka, rwkv_rk,
           rwkv_ln_w, rwkv_ln_b,
           gdn_conv_w, gdn_a_log, gdn_dt_bias, gdn_norm_w,
           w_branch, w_out, ln1_g, ln1_b,
           router_w, router_bias, moe_w1, moe_w3, moe_w2, ln2_g, ln2_b):
    alpha = (2.0 * DEPTH) ** 0.25
    bsz, n_lat, d = x.shape
    n_ctx = ctx.shape[1]
    silu_c = jax.nn.silu(c)[:, None, :]
    silu_cc = jax.nn.silu(c_ctx)
    xl, xc = x, ctx
    w_in_p, b_in_p = _bf(_pad_in_cols(w_in)), _pad_in_cols(b_in)[:, None, :]
    w1_b, w3_b, w2_b = _bf(moe_w1), _bf(moe_w3), _bf(moe_w2)
    for l in range(DEPTH):
        last = l == DEPTH - 1
        mod_l = jnp.split(_mm(silu_c, w_ada[l]) + b_ada[l], 6, axis=-1)
        mod_c = jnp.split(_mm(silu_cc, w_ada[l]) + b_ada[l], 6, axis=-1)
        h = _modulate(jnp.concatenate([xc[0], xl[0]], axis=0),
                      jnp.stack([mod_c[1], mod_l[1][0, 0]]), jnp.stack([mod_c[0], mod_l[0][0, 0]]), n_ctx)
        p = _matmul_bias(h, w_in_p, b_in_p, l, MM_TM, IN_TN)
        slab = lambda off, width: (p[None, :n_ctx, off:off + width], p[None, n_ctx:, off:off + width])
        gla_c, gla_l = slab(OFF_GLA, GLA_PROJ)
        rwkv_c, rwkv_l = slab(OFF_RWKV, RWKV_PROJ)
        gdn_c, gdn_l = slab(OFF_GDN, GDN_PROJ)
        gate_c, gate_l = slab(OFF_GATE, 3 * D_MODEL)
        oa_c, oa_l = _gla_mixer(gla_c, gla_l, gla_dec_w[l], gla_dec_b[l], gla_norm_w[l])
        ob_c, ob_l = _rwkv_mixer(rwkv_c, rwkv_l, rwkv_mu[l], rwkv_w2[l], rwkv_w0[l], rwkv_a2[l], rwkv_a0[l],
                                 rwkv_g2[l], rwkv_kk[l], rwkv_ka[l], rwkv_rk[l], rwkv_ln_w[l], rwkv_ln_b[l])
        oc_c, oc_l = _gdn_mixer(gdn_c, gdn_l, gdn_conv_w[l], gdn_a_log[l], gdn_dt_bias[l], gdn_norm_w[l])
        mix_l = _merge(gate_l, (oa_l, ob_l, oc_l), w_branch[l], w_out[l])
        xl = _layer_norm(alpha * xl + mod_l[2] * mix_l, ln1_g[l], ln1_b[l])
        h2l = xl * (1 + mod_l[4]) + mod_l[3]
        if last:
            yl = _moe_ffn(h2l.reshape(-1, d), router_w, router_bias, w1_b, w3_b, w2_b, l).reshape(xl.shape)
        else:
            mix_c = _merge(gate_c, (oa_c, ob_c, oc_c), w_branch[l], w_out[l])
            xc = _layer_norm(alpha * xc + mod_c[2] * mix_c, ln1_g[l], ln1_b[l])
            h2c = xc * (1 + mod_c[4]) + mod_c[3]
            y = _moe_ffn(jnp.concatenate([h2c, h2l], axis=1).reshape(-1, d), router_w, router_bias,
                         w1_b, w3_b, w2_b, l).reshape(bsz, n_ctx + n_lat, d)
            xc = _layer_norm(alpha * xc + mod_c[5] * y[:, :n_ctx], ln2_g[l], ln2_b[l])
            yl = y[:, n_ctx:]
        xl = _layer_norm(alpha * xl + mod_l[5] * yl, ln2_g[l], ln2_b[l])
    return xl
```

```python
import functools
import math

import jax
import jax.numpy as jnp
from jax import lax
from jax.experimental import pallas as pl
from jax.experimental.pallas import tpu as pltpu

D_MODEL = 2048
DEPTH = 4
GRID_W = 64
MIX_W = D_MODEL // 2
CHUNK = 64
GLA_HEADS = 4
GLA_DV = MIX_W // GLA_HEADS
GLA_DK = GLA_DV // 2
GLA_LR = 16
GLA_GATE_NORM = 16.0
RWKV_HEAD = 64
RWKV_HEADS = MIX_W // RWKV_HEAD
DECAY_LORA = 64
ICLR_LORA = 64
GATE_LORA = 128
RWKV_GN_EPS = 64e-5
GDN_HEAD = 128
GDN_HEADS = MIX_W // GDN_HEAD
CONV_K = 3
N_EXPERTS = 16
N_GROUPS = 4
TOP_K = 2
D_EXPERT = 1408
DISPATCH_BLOCK = 256

GLA_COLS = (GLA_HEADS * GLA_DK, GLA_HEADS * GLA_DK, MIX_W, MIX_W, GLA_LR, GLA_LR)
RWKV_COLS = (MIX_W, MIX_W, MIX_W, DECAY_LORA, DECAY_LORA, ICLR_LORA, ICLR_LORA, GATE_LORA)
GDN_COLS = (3 * MIX_W, MIX_W, GDN_HEADS, GDN_HEADS, GDN_HEADS, GDN_HEADS)
GLA_PROJ = sum(GLA_COLS)
RWKV_PROJ = sum(RWKV_COLS)
GDN_PROJ = sum(GDN_COLS)

LANE = 128
VMEM_LIMIT = 56 * 1024 * 1024
HIGHEST = lax.Precision.HIGHEST
NEG_BIG = -1e30


def _bf(x):
    return x.astype(jnp.bfloat16)


def _mm(a, b):
    return jnp.matmul(a, b)


def _split_cols(p, widths):
    out, s = [], 0
    for w in widths:
        out.append(p[..., s:s + w])
        s += w
    return out


def _layer_norm(x, g, b, eps=1e-5):
    xf = x.astype(jnp.float32)
    mu = xf.mean(-1, keepdims=True)
    var = jnp.square(xf - mu).mean(-1, keepdims=True)
    return ((xf - mu) * lax.rsqrt(var + eps)).astype(x.dtype) * g + b


def _rms_norm(x, w, eps=1e-6):
    xf = x.astype(jnp.float32)
    return (xf * lax.rsqrt(jnp.square(xf).mean(-1, keepdims=True) + eps)).astype(x.dtype) * w


def _l2_normalize(x, eps=1e-6):
    xf = x.astype(jnp.float32)
    return (xf * lax.rsqrt(jnp.square(xf).sum(-1, keepdims=True) + eps)).astype(x.dtype)


def _token_shift(p):
    pad = jnp.pad(p, ((0, 0), (1, 1), (0, 0)))
    return 0.5 * (pad[:, :-2] + pad[:, 2:])


def _grid_conv(t, w):
    b, n, ch = t.shape
    rows = n // GRID_W
    y = lax.conv_general_dilated(t.reshape(b, rows, GRID_W, ch), w[:, :, None, :], (1, 1), 'SAME',
                                 dimension_numbers=('NHWC', 'HWIO', 'NHWC'), feature_group_count=ch)
    return y.reshape(b, n, ch)


def _seq_conv(t, w):
    return lax.conv_general_dilated(t, w[:, None, :], (1,), 'SAME',
                                    dimension_numbers=('NWC', 'WIO', 'NWC'), feature_group_count=t.shape[-1])


def _bmm(a, b):
    return jnp.einsum('hmk,hkn->hmn', _bf(a), _bf(b), preferred_element_type=jnp.float32)


def _bmm_nt(a, b):
    return jnp.einsum('hmk,hnk->hmn', _bf(a), _bf(b), preferred_element_type=jnp.float32)


def _bmm_tn(a, b):
    return jnp.einsum('hkm,hkn->hmn', _bf(a), _bf(b), preferred_element_type=jnp.float32)


def _chunk_masks(reverse):
    c = CHUNK
    row = lax.broadcasted_iota(jnp.int32, (c, c), 0)
    col = lax.broadcasted_iota(jnp.int32, (c, c), 1)
    if reverse:
        return row <= col, row < col, c // 2, 0
    return row >= col, row > col, c // 2 - 1, c - 1


def _running_sum(x, incl):
    h = x.shape[0]
    hi = _bf(x)
    rest = x - hi.astype(jnp.float32)
    mid = _bf(rest)
    lo = _bf(rest - mid.astype(jnp.float32))
    tri = jnp.broadcast_to(jnp.where(incl, 1.0, 0.0).astype(jnp.bfloat16), (3 * h, CHUNK, CHUNK))
    s = jnp.einsum('hts,hsn->htn', tri, jnp.concatenate([hi, mid, lo], axis=0),
                   preferred_element_type=jnp.float32)
    return s[:h] + s[h:2 * h] + s[2 * h:]


def _unit_triangular_inverse(a_strict):
    c = a_strict.shape[-1]
    eye = (lax.broadcasted_iota(jnp.int32, (c, c), 0) == lax.broadcasted_iota(jnp.int32, (c, c), 1))
    q = -a_strict
    m = jnp.where(eye, 1.0, 0.0) + q
    q = _bmm(q, q)
    for _ in range(int(math.log2(c)) - 2):
        mq = _bmm(q, jnp.concatenate([m, q], axis=-1))
        m = m + mq[..., :c]
        q = mq[..., c:]
    return m + _bmm(q, m)


def _column_of(row_vec, n):
    return jnp.swapaxes(jnp.broadcast_to(row_vec, (row_vec.shape[0], n, n)), 1, 2)


def _rwkv_chain(shared, per_dir, t0, reverse):
    r, v, kk = shared
    lw, kd, b = per_dir
    c = CHUNK
    incl, strict, i_ref, i_last = _chunk_masks(reverse)
    cum = _running_sum(lw, incl)
    ref = cum[:, i_ref:i_ref + 1, :]
    last = cum[:, i_last:i_last + 1, :]
    rel = cum - ref
    e_inv = jnp.exp(-rel)
    p_ref = jnp.exp(ref)
    p_end = jnp.exp(last - ref)
    x = jnp.concatenate([kk * jnp.exp(rel - lw), r * jnp.exp(rel)], axis=1)
    y = jnp.concatenate([kd * e_inv, b * e_inv], axis=1)
    g = _bmm_nt(x, y)
    g_k = jnp.where(jnp.concatenate([strict, incl], axis=0), g[:, :, :c], 0.0)
    a_b = jnp.where(strict, g[:, :c, c:], 0.0)
    r_b = jnp.where(incl, g[:, c:, c:], 0.0)
    m = _unit_triangular_inverse(a_b)
    s = _bmm(x * p_ref, t0) + _bmm(g_k, v)
    u = _bmm(m, s[:, :c])
    out = s[:, c:] - _bmm(r_b, u)
    t1 = _column_of(jnp.exp(last), r.shape[2]) * t0 + _bmm_tn(y * p_end, jnp.concatenate([v, -u], axis=1))
    return out, t1


def _gla_chain(shared, per_dir, s0, reverse):
    q, k, v = shared
    (g,) = per_dir
    incl, _, i_ref, i_last = _chunk_masks(reverse)
    cum = _running_sum(g, incl)
    ref = cum[:, i_ref:i_ref + 1, :]
    last = cum[:, i_last:i_last + 1, :]
    att = jnp.where(incl, _bmm_nt(q * jnp.exp(cum - ref), k * jnp.exp(ref - cum)), 0.0)
    o = _bmm(att, v) + _bmm(q * jnp.exp(cum), s0)
    dk, dv = s0.shape[1], s0.shape[2]
    dec = _column_of(jnp.exp(last), dk)
    s1 = jnp.concatenate([dec] * (dv // dk), axis=-1) * s0 + _bmm_tn(k * jnp.exp(last - cum), v)
    return o, s1


def _gdn_chain(shared, per_dir, s0, reverse):
    q, k, v = shared
    la, beta = per_dir
    c = CHUNK
    incl, strict, _, i_last = _chunk_masks(reverse)
    gam = _running_sum(la, incl)
    last = gam[:, i_last:i_last + 1, :]
    gcol = gam[:, :, :c]
    decay = jnp.exp(jnp.where(incl, gcol - jnp.swapaxes(gcol, 1, 2), NEG_BIG))
    e_gam = jnp.exp(gam)
    a_mat = jnp.where(strict, beta[:, :, :c] * _bmm_nt(k, k) * decay, 0.0)
    rhs = jnp.concatenate([v * beta, k * (beta * e_gam)], axis=-1)
    sol = _bmm(_unit_triangular_inverse(a_mat), rhs)
    dv = v.shape[-1]
    u, w = sol[..., :dv], sol[..., dv:]
    qk = _bmm_nt(q, k) * decay
    delta = u - _bmm(w, s0)
    o = _bmm(q * e_gam, s0) + _bmm(qk, delta)
    s1 = jnp.exp(last) * s0 + _bmm_tn(k * jnp.exp(last - gam), delta)
    return o, s1


def _scan_kernel(*refs, chain, n_shared, n_dir):
    n_in = n_shared + n_dir
    f_in, b_in = refs[:n_in], refs[n_in:2 * n_in]
    yf, yb, sf, sb = refs[2 * n_in:]

    @pl.when(pl.program_id(1) == 0)
    def _():
        sf[...] = jnp.zeros_like(sf)
        sb[...] = jnp.zeros_like(sb)

    for ins, y, s, reverse in ((f_in, yf, sf, False), (b_in, yb, sb, True)):
        vals = [t[...] for t in ins]
        out, s1 = chain(vals[:n_shared], vals[n_shared:], s[...], reverse)
        y[...] = out
        s[...] = s1


def _bidir_scan(chain, shared, fwd, bwd, out_dim, state_dims, head_group, n_ctx, name):
    h, n, _ = shared[0].shape
    n_chunks = n // CHUNK
    ctx_chunks = n_ctx // CHUNK

    def fwd_map(g, i):
        return (g, i, 0)

    def bwd_map(g, i):
        return (g, jnp.where(i < ctx_chunks, ctx_chunks - 1 - i, n_chunks - 1 + ctx_chunks - i), 0)

    def specs(arrays, index_map):
        return [pl.BlockSpec((head_group, CHUNK, a.shape[2]), index_map) for a in arrays]

    ins = list(shared) + list(fwd)
    out = jax.ShapeDtypeStruct((h, n, out_dim), jnp.float32)
    state = pltpu.VMEM((head_group,) + state_dims, jnp.float32)
    return pl.pallas_call(
        functools.partial(_scan_kernel, chain=chain, n_shared=len(shared), n_dir=len(fwd)),
        grid=(h // head_group, n_chunks),
        in_specs=specs(ins, fwd_map) + specs(ins, bwd_map),
        out_specs=[pl.BlockSpec((head_group, CHUNK, out_dim), fwd_map),
                   pl.BlockSpec((head_group, CHUNK, out_dim), bwd_map)],
        out_shape=[out, out],
        scratch_shapes=[state, state],
        compiler_params=pltpu.CompilerParams(dimension_semantics=("arbitrary", "arbitrary"),
                                             vmem_limit_bytes=VMEM_LIMIT),
        name=name,
    )(*shared, *fwd, *shared, *bwd)


def _token_scan_kernel(*refs, chain, n_shared, n_dir, heads):
    n_in = n_shared + n_dir
    f_in, b_in = refs[:n_in], refs[n_in:2 * n_in]
    yf, yb, sf, sb = refs[2 * n_in:]

    @pl.when(pl.program_id(0) == 0)
    def _():
        sf[...] = jnp.zeros_like(sf)
        sb[...] = jnp.zeros_like(sb)

    def split(x):
        d = x.shape[1] // heads
        return jnp.stack([x[:, h * d:(h + 1) * d] for h in range(heads)], axis=0)

    for ins, y, s, reverse in ((f_in, yf, sf, False), (b_in, yb, sb, True)):
        vals = [split(t[...]) for t in ins]
        out, s1 = chain(vals[:n_shared], vals[n_shared:], s[...], reverse)
        y[...] = jnp.concatenate([out[h] for h in range(heads)], axis=-1)
        s[...] = s1


def _bidir_scan_tokens(chain, shared, fwd, bwd, heads, out_dim, state_dims, n_ctx, name):
    n = shared[0].shape[0]
    n_chunks = n // CHUNK
    ctx_chunks = n_ctx // CHUNK

    def fwd_map(i):
        return (i, 0)

    def bwd_map(i):
        return (jnp.where(i < ctx_chunks, ctx_chunks - 1 - i, n_chunks - 1 + ctx_chunks - i), 0)

    def specs(arrays, index_map):
        return [pl.BlockSpec((CHUNK, a.shape[1]), index_map) for a in arrays]

    ins = list(shared) + list(fwd)
    out = jax.ShapeDtypeStruct((n, heads * out_dim), jnp.float32)
    state = pltpu.VMEM((heads,) + state_dims, jnp.float32)
    return pl.pallas_call(
        functools.partial(_token_scan_kernel, chain=chain, n_shared=len(shared), n_dir=len(fwd), heads=heads),
        grid=(n_chunks,),
        in_specs=specs(ins, fwd_map) + specs(ins, bwd_map),
        out_specs=[pl.BlockSpec((CHUNK, heads * out_dim), fwd_map),
                   pl.BlockSpec((CHUNK, heads * out_dim), bwd_map)],
        out_shape=[out, out],
        scratch_shapes=[state, state],
        compiler_params=pltpu.CompilerParams(dimension_semantics=("arbitrary",), vmem_limit_bytes=VMEM_LIMIT),
        name=name,
    )(*shared, *fwd, *shared, *bwd)


def _token_rows(tc, tl):
    t = jnp.concatenate([tc[0], tl[0]], axis=0)
    return t.reshape(t.shape[0], -1)


def _head_major(tc, tl):
    return jnp.transpose(jnp.concatenate([tc[0], tl[0]], axis=0), (1, 0, 2))


def _token_major(y, n_ctx):
    y = jnp.transpose(y, (1, 0, 2))[None]
    return y[:, :n_ctx], y[:, n_ctx:]


SEG_ROWS = 256
MM_TM = 1056
IN_TN = 896


def _modulate_kernel(x, scale, shift, o):
    o[...] = (x[...] * (1.0 + scale[0]) + shift[0]).astype(o.dtype)


def _modulate(x, scale, shift, n_ctx):
    n, d = x.shape
    assert n_ctx == SEG_ROWS and n % SEG_ROWS == 0
    seg = lambda i: (jnp.minimum(i, 1), 0, 0)
    return pl.pallas_call(
        _modulate_kernel,
        grid=(n // SEG_ROWS,),
        in_specs=[pl.BlockSpec((SEG_ROWS, d), lambda i: (i, 0)),
                  pl.BlockSpec((1, 1, d), seg), pl.BlockSpec((1, 1, d), seg)],
        out_specs=pl.BlockSpec((SEG_ROWS, d), lambda i: (i, 0)),
        out_shape=jax.ShapeDtypeStruct((n, d), jnp.bfloat16),
        compiler_params=pltpu.CompilerParams(dimension_semantics=("arbitrary",), vmem_limit_bytes=VMEM_LIMIT),
        name="modulate",
    )(x, scale[:, None, :], shift[:, None, :])


def _matmul_bias_kernel(x, w, b, o):
    o[...] = jnp.dot(x[...], w[0], preferred_element_type=jnp.float32) + b[0]


def _matmul_bias(x, w, b, layer, tm, tn):
    m, k = x.shape
    n = w.shape[2]
    assert m % tm == 0 and n % tn == 0
    return pl.pallas_call(
        _matmul_bias_kernel,
        grid=(m // tm, n // tn),
        in_specs=[pl.BlockSpec((tm, k), lambda i, j: (i, 0)),
                  pl.BlockSpec((1, k, tn), lambda i, j: (layer, 0, j)),
                  pl.BlockSpec((1, 1, tn), lambda i, j: (layer, 0, j))],
        out_specs=pl.BlockSpec((tm, tn), lambda i, j: (i, j)),
        out_shape=jax.ShapeDtypeStruct((m, n), jnp.float32),
        compiler_params=pltpu.CompilerParams(dimension_semantics=("arbitrary", "arbitrary"),
                                             vmem_limit_bytes=VMEM_LIMIT),
        name="matmul_bias",
    )(x, w, b)


_GLA_MAIN = GLA_PROJ - 2 * GLA_LR
_GDN_MAIN = GDN_PROJ - 4 * GDN_HEADS
_LR_PAD = LANE - 2 * GLA_LR
_AB_PAD = LANE - 4 * GDN_HEADS
OFF_GLA = 0
OFF_RWKV = _GLA_MAIN + LANE
OFF_GDN = OFF_RWKV + RWKV_PROJ
OFF_GATE = OFF_GDN + _GDN_MAIN + LANE
IN_PAD = OFF_GATE + 3 * D_MODEL


def _pad_in_cols(w):
    o_rwkv, o_gdn, o_gate = GLA_PROJ, GLA_PROJ + RWKV_PROJ, GLA_PROJ + RWKV_PROJ + GDN_PROJ
    z = lambda n: jnp.zeros(w.shape[:-1] + (n,), w.dtype)
    return jnp.concatenate([w[..., :o_rwkv], z(_LR_PAD), w[..., o_rwkv:o_gdn], w[..., o_gdn:o_gate], z(_AB_PAD),
                            w[..., o_gate:]], axis=-1)


def _gla_mixer(pc, pl_, dec_w2, dec_b, norm_w):
    n_ctx = pc.shape[1]

    def prep(p):
        bsz, n = p.shape[:2]
        q, k, v, og, lr_f, lr_b = _split_cols(p, GLA_COLS)
        heads = lambda t, d: t.reshape(bsz, n, GLA_HEADS, d)
        g = [heads(jax.nn.log_sigmoid((_mm(lr, dec_w2[d]) + dec_b[d]).astype(jnp.float32)) / GLA_GATE_NORM, GLA_DK)
             for d, lr in enumerate((lr_f, lr_b))]
        return heads(q, GLA_DK) * GLA_DK ** -0.5, heads(k, GLA_DK), heads(v, GLA_DV), og, g

    qc, kc, vc, ogc, gc = prep(pc)
    ql, kl, vl, ogl, gl = prep(pl_)
    of, ob = _bidir_scan_tokens(_gla_chain, (_token_rows(qc, ql), _token_rows(kc, kl), _token_rows(vc, vl)),
                                (_token_rows(gc[0], gl[0]),), (_token_rows(gc[1], gl[1]),),
                                GLA_HEADS, GLA_DV, (GLA_DK, GLA_DV), n_ctx, "gla_scan")
    o = (of + ob).reshape(1, -1, GLA_HEADS, GLA_DV)
    oc, ol = o[:, :n_ctx], o[:, n_ctx:]

    def finish(o, og):
        return _rms_norm(o, norm_w).reshape(og.shape) * jax.nn.silu(og)

    return finish(oc, ogc), finish(ol, ogl)


def _rwkv_mixer(pc, pl_, mu, w2, w0, a2, a0, g2, k_k, k_a, r_k, ln_w, ln_b):
    n_ctx = pc.shape[1]

    def prep(p):
        bsz, n = p.shape[:2]
        heads = lambda t: t.reshape(bsz, n, RWKV_HEADS, RWKV_HEAD)
        p = p + (_token_shift(p) - p) * mu
        r, k, v, wl_f, wl_b, al_f, al_b, gl = _split_cols(p, RWKV_COLS)
        kk = _l2_normalize(heads(k * k_k))
        per_dir = []
        for d, (wl, al) in enumerate(((wl_f, al_f), (wl_b, al_b))):
            z = w0[d] + _mm(jnp.tanh(wl), w2[d])
            log_decay = -jnp.exp(-jax.nn.softplus(-z) - 0.5)
            a = jax.nn.sigmoid(a0[d] + _mm(al, a2[d]))
            per_dir.append((heads(log_decay), heads(k * (1 + (a - 1) * k_a)), kk * heads(a)))
        g = _mm(jax.nn.sigmoid(gl), g2)
        return heads(r), heads(k), heads(v), kk, per_dir, g

    rc, kc, vc, kkc, dc, gc = prep(pc)
    rl, kl, vl, kkl, dl, gl = prep(pl_)
    yf, yb = _bidir_scan_tokens(_rwkv_chain, (_token_rows(rc, rl), _token_rows(vc, vl), _token_rows(kkc, kkl)),
                                tuple(_token_rows(tc, tl) for tc, tl in zip(dc[0], dl[0])),
                                tuple(_token_rows(tc, tl) for tc, tl in zip(dc[1], dl[1])),
                                RWKV_HEADS, RWKV_HEAD, (RWKV_HEAD, RWKV_HEAD), n_ctx, "rwkv7_scan")
    y = (yf + yb).reshape(1, -1, RWKV_HEADS, RWKV_HEAD)
    yc, yl = y[:, :n_ctx], y[:, n_ctx:]

    def finish(y, r, k, v, g):
        yf = y.astype(jnp.float32)
        m = yf.mean(-1, keepdims=True)
        var = jnp.square(yf - m).mean(-1, keepdims=True)
        y = ((yf - m) * lax.rsqrt(var + RWKV_GN_EPS)).astype(y.dtype)
        y = y * ln_w.reshape(RWKV_HEADS, RWKV_HEAD) + ln_b.reshape(RWKV_HEADS, RWKV_HEAD)
        y = y + jnp.sum(r * k * r_k, axis=-1, keepdims=True) * v
        return y.reshape(g.shape) * g

    return finish(yc, rc, kc, vc, gc), finish(yl, rl, kl, vl, gl)


GDN_DEFAULT_PRECISION_WEIGHT = 0.75


def _to_chunks(t):
    b, n = t.shape[:2]
    t = t.reshape((b, n // CHUNK, CHUNK) + t.shape[2:])
    return jnp.swapaxes(t, 2, 3)


def _from_chunks(t):
    t = jnp.swapaxes(t, 2, 3)
    return t.reshape((t.shape[0], t.shape[1] * t.shape[2]) + t.shape[3:])


def _two_stage(scan_fn, ctx_args, lat_args, state0, reverse):
    if reverse:
        ctx_args = tuple(jnp.flip(t, axis=1) for t in ctx_args)
        lat_args = tuple(jnp.flip(t, axis=1) for t in lat_args)
    o_ctx, s_ctx = scan_fn(*ctx_args, state0)
    o_lat, _ = scan_fn(*lat_args, s_ctx)
    if reverse:
        o_ctx, o_lat = jnp.flip(o_ctx, axis=1), jnp.flip(o_lat, axis=1)
    return o_ctx, o_lat


def _gdn_chunked(q, k, v, log_a, beta, s0):
    dt = v.dtype
    f32 = jnp.float32
    q, k, v, log_a, beta = (_to_chunks(t.astype(f32)) for t in (q, k, v, log_a, beta))
    gam = jnp.cumsum(log_a, axis=-1)
    incl = jnp.tril(jnp.ones((CHUNK, CHUNK), bool))
    strict = jnp.tril(jnp.ones((CHUNK, CHUNK), bool), -1)
    decay = jnp.exp(jnp.where(incl, gam[..., :, None] - gam[..., None, :], -jnp.inf))
    a_mat = jnp.where(strict, beta[..., :, None] * jnp.einsum('bnhid,bnhjd->bnhij', k, k) * decay, 0.0)
    rhs = jnp.concatenate([v * beta[..., None], k * (beta * jnp.exp(gam))[..., None]], axis=-1)
    sol = lax.linalg.triangular_solve(a_mat, rhs, left_side=True, lower=True, unit_diagonal=True)
    dv = v.shape[-1]
    u, w = sol[..., :dv], sol[..., dv:]
    qk = jnp.einsum('bnhid,bnhjd->bnhij', q, k) * decay
    q_dec = q * jnp.exp(gam)[..., None]
    k_end = k * jnp.exp(gam[..., -1:] - gam)[..., None]
    dec = jnp.exp(gam[..., -1])

    def step(s, inp):
        u_n, w_n, qk_n, q_n, k_n, dec_n = inp
        delta = u_n - jnp.einsum('bhcd,bhdv->bhcv', w_n, s)
        o_n = jnp.einsum('bhcd,bhdv->bhcv', q_n, s) + jnp.einsum('bhij,bhjv->bhiv', qk_n, delta)
        s = dec_n[..., None, None] * s + jnp.einsum('bhcd,bhcv->bhdv', k_n, delta)
        return s, o_n

    xs = tuple(jnp.moveaxis(t, 1, 0) for t in (u, w, qk, q_dec, k_end, dec))
    s_fin, o = lax.scan(step, s0, xs)
    return _from_chunks(jnp.moveaxis(o, 0, 1)).astype(dt), s_fin


def _gdn_mixer(pc, pl_, conv_w, a_log, dt_bias, norm_w):
    n_ctx = pc.shape[1]

    def prep(p, on_grid):
        bsz, n = p.shape[:2]
        heads = lambda t: t.reshape(bsz, n, GDN_HEADS, GDN_HEAD)
        qkv, z, a_f, a_b, b_f, b_b = _split_cols(p, GDN_COLS)
        qkv = jax.nn.silu(_grid_conv(qkv, conv_w) if on_grid else _seq_conv(qkv, conv_w[CONV_K // 2]))
        q, k, v = jnp.split(qkv, 3, axis=-1)
        per_dir = [(-jnp.exp(a_log[d]) * jax.nn.softplus(a + dt_bias[d]), jax.nn.sigmoid(b))
                   for d, (a, b) in enumerate(((a_f, b_f), (a_b, b_b)))]
        return _l2_normalize(heads(q)) * GDN_HEAD ** -0.5, _l2_normalize(heads(k)), heads(v), z, per_dir

    qc, kc, vc, zc, dc = prep(pc, False)
    ql, kl, vl, zl, dl = prep(pl_, True)

    def lanes(tc, tl):
        t = jnp.transpose(jnp.concatenate([tc[0], tl[0]], axis=0), (1, 0))
        return jnp.broadcast_to(t[:, :, None], t.shape + (GDN_HEAD,))

    of, ob = _bidir_scan(_gdn_chain, (_head_major(qc, ql), _head_major(kc, kl), _head_major(vc, vl)),
                         tuple(lanes(tc, tl) for tc, tl in zip(dc[0], dl[0])),
                         tuple(lanes(tc, tl) for tc, tl in zip(dc[1], dl[1])),
                         GDN_HEAD, (GDN_HEAD, GDN_HEAD), GDN_HEADS, n_ctx, "gdn_scan")
    pc_o, pl_o = _token_major(of + ob, n_ctx)
    s0 = jnp.zeros((pc.shape[0], GDN_HEADS, GDN_HEAD, GDN_HEAD), jnp.float32)
    cf, lf = _two_stage(_gdn_chunked, (qc, kc, vc) + dc[0], (ql, kl, vl) + dl[0], s0, False)
    cb, lb = _two_stage(_gdn_chunked, (qc, kc, vc) + dc[1], (ql, kl, vl) + dl[1], s0, True)
    mix = lambda a, b: GDN_DEFAULT_PRECISION_WEIGHT * a + (1.0 - GDN_DEFAULT_PRECISION_WEIGHT) * b
    oc, ol = mix(cf + cb, pc_o), mix(lf + lb, pl_o)

    def finish(o, z):
        return _rms_norm(o, norm_w).reshape(z.shape) * jax.nn.silu(z)

    return finish(oc, zc), finish(ol, zl)


def _merge(gate_pre, outs, w_branch, w_out):
    gates = jax.nn.sigmoid(gate_pre)
    y = 0
    for m, o in enumerate(outs):
        y = y + gates[..., m * D_MODEL:(m + 1) * D_MODEL] * _mm(o, w_branch[m])
    return _mm(y, w_out)


def _expert_kernel(blk_e, n_used, x, w1, w3, w2, o):
    i = pl.program_id(0)

    @pl.when(i < n_used[0])
    def _():
        xe = x[...]
        h1 = jnp.dot(xe, w1[0, 0], preferred_element_type=jnp.float32)
        h3 = jnp.dot(xe, w3[0, 0], preferred_element_type=jnp.float32)
        o[...] = jnp.dot(_bf(jax.nn.silu(h1) * h3), w2[0, 0], preferred_element_type=jnp.float32)

    @pl.when(i >= n_used[0])
    def _():
        o[...] = jnp.zeros_like(o)


def _expert_blocks(blk_e, n_used, xb, w1, w3, w2, layer):
    n_slots, d = xb.shape
    de = w1.shape[3]
    n_blk = n_slots // DISPATCH_BLOCK
    grid_spec = pltpu.PrefetchScalarGridSpec(
        num_scalar_prefetch=2,
        grid=(n_blk,),
        in_specs=[pl.BlockSpec((DISPATCH_BLOCK, d), lambda i, be, nu: (i, 0)),
                  pl.BlockSpec((1, 1, d, de), lambda i, be, nu: (layer, be[i], 0, 0)),
                  pl.BlockSpec((1, 1, d, de), lambda i, be, nu: (layer, be[i], 0, 0)),
                  pl.BlockSpec((1, 1, de, d), lambda i, be, nu: (layer, be[i], 0, 0))],
        out_specs=pl.BlockSpec((DISPATCH_BLOCK, d), lambda i, be, nu: (i, 0)),
    )
    return pl.pallas_call(
        _expert_kernel,
        grid_spec=grid_spec,
        out_shape=jax.ShapeDtypeStruct((n_slots, d), jnp.float32),
        compiler_params=pltpu.CompilerParams(dimension_semantics=("arbitrary",), vmem_limit_bytes=VMEM_LIMIT),
        name="moe_experts",
    )(blk_e, n_used, xb, w1, w3, w2)


def _moe_ffn(h, router_w, router_bias, w1, w3, w2, layer):
    n_tok, d = h.shape
    scores = jax.nn.sigmoid(_mm(h, router_w))
    sel = scores + router_bias.astype(jnp.float32)
    grp_top = lax.top_k(sel.reshape(n_tok, N_GROUPS, N_EXPERTS // N_GROUPS), 2)[0]
    best_group = jnp.argmax(grp_top.sum(-1), axis=-1)
    in_group = (jnp.arange(N_EXPERTS) // (N_EXPERTS // N_GROUPS))[None, :] == best_group[:, None]
    _, top_e = lax.top_k(jnp.where(in_group, sel, -jnp.inf), TOP_K)
    top_s = jnp.take_along_axis(scores, top_e, axis=-1)
    top_w = (top_s / top_s.sum(-1, keepdims=True)).astype(h.dtype)
    n_asg = n_tok * TOP_K
    flat_e = top_e.reshape(-1)
    order = jnp.argsort(flat_e)
    sorted_e = flat_e[order]
    counts = jnp.bincount(flat_e, length=N_EXPERTS)
    padded = (counts + DISPATCH_BLOCK - 1) // DISPATCH_BLOCK * DISPATCH_BLOCK
    pad_end = jnp.cumsum(padded)
    first = jnp.cumsum(counts) - counts
    dest = (pad_end - padded)[sorted_e] + jnp.arange(n_asg) - first[sorted_e]
    n_blk = -(-n_asg // DISPATCH_BLOCK) + N_EXPERTS
    tok = jnp.repeat(jnp.arange(n_tok, dtype=jnp.int32), TOP_K)
    slot_tok = jnp.full((n_blk * DISPATCH_BLOCK,), n_tok, jnp.int32).at[dest].set(tok[order])
    blk_e = jnp.minimum(jnp.searchsorted(pad_end, jnp.arange(n_blk) * DISPATCH_BLOCK, side='right'), N_EXPERTS - 1)
    h_pad = jnp.concatenate([_bf(h), jnp.zeros((1, d), jnp.bfloat16)], axis=0)
    n_used = (pad_end[-1] // DISPATCH_BLOCK).astype(jnp.int32).reshape(1)
    yb = _expert_blocks(blk_e.astype(jnp.int32), n_used, h_pad[slot_tok], w1, w3, w2, layer)
    slot_of = jnp.zeros((n_asg,), jnp.int32).at[order].set(dest.astype(jnp.int32)).reshape(n_tok, TOP_K)
    y = 0
    for j in range(TOP_K):
        y = y + yb[slot_of[:, j]] * top_w[:, j:j + 1]
    return y


def kernel(x, c, ctx, c_ctx, w_ada, b_ada, w_in, b_in,
           gla_dec_w, gla_dec_b, gla_norm_w,
           rwkv_mu, rwkv_w2, rwkv_w0, rwkv_a2, rwkv_a0, rwkv_g2, rwkv_kk, rwkv_ka, rwkv_rk,
           rwkv_ln_w, rwkv_ln_b,
           gdn_conv_w, gdn_a_log, gdn_dt_bias, gdn_norm_w,
           w_branch, w_out, ln1_g, ln1_b,
           router_w, router_bias, moe_w1, moe_w3, moe_w2, ln2_g, ln2_b):
    alpha = (2.0 * DEPTH) ** 0.25
    bsz, n_lat, d = x.shape
    n_ctx = ctx.shape[1]
    silu_c = jax.nn.silu(c)[:, None, :]
    silu_cc = jax.nn.silu(c_ctx)
    xl, xc = x, ctx
    w_in_p, b_in_p = _bf(_pad_in_cols(w_in)), _pad_in_cols(b_in)[:, None, :]
    w1_b, w3_b, w2_b = _bf(moe_w1), _bf(moe_w3), _bf(moe_w2)
    for l in range(DEPTH):
        last = l == DEPTH - 1
        mod_l = jnp.split(_mm(silu_c, w_ada[l]) + b_ada[l], 6, axis=-1)
        mod_c = jnp.split(_mm(silu_cc, w_ada[l]) + b_ada[l], 6, axis=-1)
        h = _modulate(jnp.concatenate([xc[0], xl[0]], axis=0),
                      jnp.stack([mod_c[1], mod_l[1][0, 0]]), jnp.stack([mod_c[0], mod_l[0][0, 0]]), n_ctx)
        p = _matmul_bias(h, w_in_p, b_in_p, l, MM_TM, IN_TN)
        slab = lambda off, width: (p[None, :n_ctx, off:off + width], p[None, n_ctx:, off:off + width])
        gla_c, gla_l = slab(OFF_GLA, GLA_PROJ)
        rwkv_c, rwkv_l = slab(OFF_RWKV, RWKV_PROJ)
        gdn_c, gdn_l = slab(OFF_GDN, GDN_PROJ)
        gate_c, gate_l = slab(OFF_GATE, 3 * D_MODEL)
        oa_c, oa_l = _gla_mixer(gla_c, gla_l, gla_dec_w[l], gla_dec_b[l], gla_norm_w[l])
        ob_c, ob_l = _rwkv_mixer(rwkv_c, rwkv_l, rwkv_mu[l], rwkv_w2[l], rwkv_w0[l], rwkv_a2[l], rwkv_a0[l],
                                 rwkv_g2[l], rwkv_kk[l], rwkv_ka[l], rwkv_rk[l], rwkv_ln_w[l], rwkv_ln_b[l])
        oc_c, oc_l = _gdn_mixer(gdn_c, gdn_l, gdn_conv_w[l], gdn_a_log[l], gdn_dt_bias[l], gdn_norm_w[l])
        mix_l = _merge(gate_l, (oa_l, ob_l, oc_l), w_branch[l], w_out[l])
        xl = _layer_norm(alpha * xl + mod_l[2] * mix_l, ln1_g[l], ln1_b[l])
        h2l = xl * (1 + mod_l[4]) + mod_l[3]
        if last:
            yl = _moe_ffn(h2l.reshape(-1, d), router_w, router_bias, w1_b, w3_b, w2_b, l).reshape(xl.shape)
        else:
            mix_c = _merge(gate_c, (oa_c, ob_c, oc_c), w_branch[l], w_out[l])
            xc = _layer_norm(alpha * xc + mod_c[2] * mix_c, ln1_g[l], ln1_b[l])
            h2c = xc * (1 + mod_c[4]) + mod_c[3]
            y = _moe_ffn(jnp.concatenate([h2c, h2l], axis=1).reshape(-1, d), router_w, router_bias,
                         w1_b, w3_b, w2_b, l).reshape(bsz, n_ctx + n_lat, d)
            xc = _layer_norm(alpha * xc + mod_c[5] * y[:, :n_ctx], ln2_g[l], ln2_b[l])
            yl = y[:, n_ctx:]
        xl = _layer_norm(alpha * xl + mod_l[5] * yl, ln2_g[l], ln2_b[l])
    return xl
```

```python
import functools
import math

import jax
import jax.numpy as jnp
from jax import lax
from jax.experimental import pallas as pl
from jax.experimental.pallas import tpu as pltpu

D_MODEL = 2048
DEPTH = 4
GRID_W = 64
MIX_W = D_MODEL // 2
CHUNK = 64
GLA_HEADS = 4
GLA_DV = MIX_W // GLA_HEADS
GLA_DK = GLA_DV // 2
GLA_LR = 16
GLA_GATE_NORM = 16.0
RWKV_HEAD = 64
RWKV_HEADS = MIX_W // RWKV_HEAD
DECAY_LORA = 64
ICLR_LORA = 64
GATE_LORA = 128
RWKV_GN_EPS = 64e-5
GDN_HEAD = 128
GDN_HEADS = MIX_W // GDN_HEAD
CONV_K = 3
N_EXPERTS = 16
N_GROUPS = 4
TOP_K = 2
D_EXPERT = 1408
DISPATCH_BLOCK = 256

GLA_COLS = (GLA_HEADS * GLA_DK, GLA_HEADS * GLA_DK, MIX_W, MIX_W, GLA_LR, GLA_LR)
RWKV_COLS = (MIX_W, MIX_W, MIX_W, DECAY_LORA, DECAY_LORA, ICLR_LORA, ICLR_LORA, GATE_LORA)
GDN_COLS = (3 * MIX_W, MIX_W, GDN_HEADS, GDN_HEADS, GDN_HEADS, GDN_HEADS)
GLA_PROJ = sum(GLA_COLS)
RWKV_PROJ = sum(RWKV_COLS)
GDN_PROJ = sum(GDN_COLS)

LANE = 128
VMEM_LIMIT = 56 * 1024 * 1024
HIGHEST = lax.Precision.HIGHEST
NEG_BIG = -1e30


def _bf(x):
    return x.astype(jnp.bfloat16)


def _mm(a, b):
    return jnp.matmul(a, b)


def _split_cols(p, widths):
    out, s = [], 0
    for w in widths:
        out.append(p[..., s:s + w])
        s += w
    return out


def _layer_norm(x, g, b, eps=1e-5):
    xf = x.astype(jnp.float32)
    mu = xf.mean(-1, keepdims=True)
    var = jnp.square(xf - mu).mean(-1, keepdims=True)
    return ((xf - mu) * lax.rsqrt(var + eps)).astype(x.dtype) * g + b


def _rms_norm(x, w, eps=1e-6):
    xf = x.astype(jnp.float32)
    return (xf * lax.rsqrt(jnp.square(xf).mean(-1, keepdims=True) + eps)).astype(x.dtype) * w


def _l2_normalize(x, eps=1e-6):
    xf = x.astype(jnp.float32)
    return (xf * lax.rsqrt(jnp.square(xf).sum(-1, keepdims=True) + eps)).astype(x.dtype)


def _token_shift(p):
    pad = jnp.pad(p, ((0, 0), (1, 1), (0, 0)))
    return 0.5 * (pad[:, :-2] + pad[:, 2:])


def _grid_conv(t, w):
    b, n, ch = t.shape
    rows = n // GRID_W
    y = lax.conv_general_dilated(t.reshape(b, rows, GRID_W, ch), w[:, :, None, :], (1, 1), 'SAME',
                                 dimension_numbers=('NHWC', 'HWIO', 'NHWC'), feature_group_count=ch)
    return y.reshape(b, n, ch)


def _seq_conv(t, w):
    return lax.conv_general_dilated(t, w[:, None, :], (1,), 'SAME',
                                    dimension_numbers=('NWC', 'WIO', 'NWC'), feature_group_count=t.shape[-1])


def _bmm(a, b):
    return jnp.einsum('hmk,hkn->hmn', _bf(a), _bf(b), preferred_element_type=jnp.float32)


def _bmm_nt(a, b):
    return jnp.einsum('hmk,hnk->hmn', _bf(a), _bf(b), preferred_element_type=jnp.float32)


def _bmm_tn(a, b):
    return jnp.einsum('hkm,hkn->hmn', _bf(a), _bf(b), preferred_element_type=jnp.float32)


def _chunk_masks(reverse):
    c = CHUNK
    row = lax.broadcasted_iota(jnp.int32, (c, c), 0)
    col = lax.broadcasted_iota(jnp.int32, (c, c), 1)
    if reverse:
        return row <= col, row < col, c // 2, 0
    return row >= col, row > col, c // 2 - 1, c - 1


def _running_sum(x, incl):
    h = x.shape[0]
    hi = _bf(x)
    rest = x - hi.astype(jnp.float32)
    mid = _bf(rest)
    lo = _bf(rest - mid.astype(jnp.float32))
    tri = jnp.broadcast_to(jnp.where(incl, 1.0, 0.0).astype(jnp.bfloat16), (3 * h, CHUNK, CHUNK))
    s = jnp.einsum('hts,hsn->htn', tri, jnp.concatenate([hi, mid, lo], axis=0),
                   preferred_element_type=jnp.float32)
    return s[:h] + s[h:2 * h] + s[2 * h:]


def _unit_triangular_inverse(a_strict):
    c = a_strict.shape[-1]
    eye = (lax.broadcasted_iota(jnp.int32, (c, c), 0) == lax.broadcasted_iota(jnp.int32, (c, c), 1))
    q = -a_strict
    m = jnp.where(eye, 1.0, 0.0) + q
    q = _bmm(q, q)
    for _ in range(int(math.log2(c)) - 2):
        mq = _bmm(q, jnp.concatenate([m, q], axis=-1))
        m = m + mq[..., :c]
        q = mq[..., c:]
    return m + _bmm(q, m)


def _column_of(row_vec, n):
    return jnp.swapaxes(jnp.broadcast_to(row_vec, (row_vec.shape[0], n, n)), 1, 2)


def _rwkv_chain(shared, per_dir, t0, reverse):
    r, v, kk = shared
    lw, kd, b = per_dir
    c = CHUNK
    incl, strict, i_ref, i_last = _chunk_masks(reverse)
    cum = _running_sum(lw, incl)
    ref = cum[:, i_ref:i_ref + 1, :]
    last = cum[:, i_last:i_last + 1, :]
    rel = cum - ref
    e_inv = jnp.exp(-rel)
    p_ref = jnp.exp(ref)
    p_end = jnp.exp(last - ref)
    x = jnp.concatenate([kk * jnp.exp(rel - lw), r * jnp.exp(rel)], axis=1)
    y = jnp.concatenate([kd * e_inv, b * e_inv], axis=1)
    g = _bmm_nt(x, y)
    g_k = jnp.where(jnp.concatenate([strict, incl], axis=0), g[:, :, :c], 0.0)
    a_b = jnp.where(strict, g[:, :c, c:], 0.0)
    r_b = jnp.where(incl, g[:, c:, c:], 0.0)
    m = _unit_triangular_inverse(a_b)
    s = _bmm(x * p_ref, t0) + _bmm(g_k, v)
    u = _bmm(m, s[:, :c])
    out = s[:, c:] - _bmm(r_b, u)
    t1 = _column_of(jnp.exp(last), r.shape[2]) * t0 + _bmm_tn(y * p_end, jnp.concatenate([v, -u], axis=1))
    return out, t1


def _gla_chain(shared, per_dir, s0, reverse):
    q, k, v = shared
    (g,) = per_dir
    incl, _, i_ref, i_last = _chunk_masks(reverse)
    cum = _running_sum(g, incl)
    ref = cum[:, i_ref:i_ref + 1, :]
    last = cum[:, i_last:i_last + 1, :]
    att = jnp.where(incl, _bmm_nt(q * jnp.exp(cum - ref), k * jnp.exp(ref - cum)), 0.0)
    o = _bmm(att, v) + _bmm(q * jnp.exp(cum), s0)
    dk, dv = s0.shape[1], s0.shape[2]
    dec = _column_of(jnp.exp(last), dk)
    s1 = jnp.concatenate([dec] * (dv // dk), axis=-1) * s0 + _bmm_tn(k * jnp.exp(last - cum), v)
    return o, s1


def _gdn_chain(shared, per_dir, s0, reverse):
    q, k, v = shared
    la, beta = per_dir
    c = CHUNK
    incl, strict, _, i_last = _chunk_masks(reverse)
    gam = _running_sum(la, incl)
    last = gam[:, i_last:i_last + 1, :]
    gcol = gam[:, :, :c]
    decay = jnp.exp(jnp.where(incl, gcol - jnp.swapaxes(gcol, 1, 2), NEG_BIG))
    e_gam = jnp.exp(gam)
    a_mat = jnp.where(strict, beta[:, :, :c] * _bmm_nt(k, k) * decay, 0.0)
    rhs = jnp.concatenate([v * beta, k * (beta * e_gam)], axis=-1)
    sol = _bmm(_unit_triangular_inverse(a_mat), rhs)
    dv = v.shape[-1]
    u, w = sol[..., :dv], sol[..., dv:]
    qk = _bmm_nt(q, k) * decay
    delta = u - _bmm(w, s0)
    o = _bmm(q * e_gam, s0) + _bmm(qk, delta)
    s1 = jnp.exp(last) * s0 + _bmm_tn(k * jnp.exp(last - gam), delta)
    return o, s1


def _scan_kernel(*refs, chain, n_shared, n_dir):
    n_in = n_shared + n_dir
    f_in, b_in = refs[:n_in], refs[n_in:2 * n_in]
    yf, yb, sf, sb = refs[2 * n_in:]

    @pl.when(pl.program_id(1) == 0)
    def _():
        sf[...] = jnp.zeros_like(sf)
        sb[...] = jnp.zeros_like(sb)

    for ins, y, s, reverse in ((f_in, yf, sf, False), (b_in, yb, sb, True)):
        vals = [t[...] for t in ins]
        out, s1 = chain(vals[:n_shared], vals[n_shared:], s[...], reverse)
        y[...] = out
        s[...] = s1


def _bidir_scan(chain, shared, fwd, bwd, out_dim, state_dims, head_group, n_ctx, name):
    h, n, _ = shared[0].shape
    n_chunks = n // CHUNK
    ctx_chunks = n_ctx // CHUNK

    def fwd_map(g, i):
        return (g, i, 0)

    def bwd_map(g, i):
        return (g, jnp.where(i < ctx_chunks, ctx_chunks - 1 - i, n_chunks - 1 + ctx_chunks - i), 0)

    def specs(arrays, index_map):
        return [pl.BlockSpec((head_group, CHUNK, a.shape[2]), index_map) for a in arrays]

    ins = list(shared) + list(fwd)
    out = jax.ShapeDtypeStruct((h, n, out_dim), jnp.float32)
    state = pltpu.VMEM((head_group,) + state_dims, jnp.float32)
    return pl.pallas_call(
        functools.partial(_scan_kernel, chain=chain, n_shared=len(shared), n_dir=len(fwd)),
        grid=(h // head_group, n_chunks),
        in_specs=specs(ins, fwd_map) + specs(ins, bwd_map),
        out_specs=[pl.BlockSpec((head_group, CHUNK, out_dim), fwd_map),
                   pl.BlockSpec((head_group, CHUNK, out_dim), bwd_map)],
        out_shape=[out, out],
        scratch_shapes=[state, state],
        compiler_params=pltpu.CompilerParams(dimension_semantics=("arbitrary", "arbitrary"),
                                             vmem_limit_bytes=VMEM_LIMIT),
        name=name,
    )(*shared, *fwd, *shared, *bwd)


def _token_scan_kernel(*refs, chain, n_shared, n_dir, heads):
    n_in = n_shared + n_dir
    f_in, b_in = refs[:n_in], refs[n_in:2 * n_in]
    yf, yb, sf, sb = refs[2 * n_in:]

    @pl.when(pl.program_id(0) == 0)
    def _():
        sf[...] = jnp.zeros_like(sf)
        sb[...] = jnp.zeros_like(sb)

    def split(x):
        d = x.shape[1] // heads
        return jnp.stack([x[:, h * d:(h + 1) * d] for h in range(heads)], axis=0)

    for ins, y, s, reverse in ((f_in, yf, sf, False), (b_in, yb, sb, True)):
        vals = [split(t[...]) for t in ins]
        out, s1 = chain(vals[:n_shared], vals[n_shared:], s[...], reverse)
        y[...] = jnp.concatenate([out[h] for h in range(heads)], axis=-1)
        s[...] = s1


def _bidir_scan_tokens(chain, shared, fwd, bwd, heads, out_dim, state_dims, n_ctx, name):
    n = shared[0].shape[0]
    n_chunks = n // CHUNK
    ctx_chunks = n_ctx // CHUNK

    def fwd_map(i):
        return (i, 0)

    def bwd_map(i):
        return (jnp.where(i < ctx_chunks, ctx_chunks - 1 - i, n_chunks - 1 + ctx_chunks - i), 0)

    def specs(arrays, index_map):
        return [pl.BlockSpec((CHUNK, a.shape[1]), index_map) for a in arrays]

    ins = list(shared) + list(fwd)
    out = jax.ShapeDtypeStruct((n, heads * out_dim), jnp.float32)
    state = pltpu.VMEM((heads,) + state_dims, jnp.float32)
    return pl.pallas_call(
        functools.partial(_token_scan_kernel, chain=chain, n_shared=len(shared), n_dir=len(fwd), heads=heads),
        grid=(n_chunks,),
        in_specs=specs(ins, fwd_map) + specs(ins, bwd_map),
        out_specs=[pl.BlockSpec((CHUNK, heads * out_dim), fwd_map),
                   pl.BlockSpec((CHUNK, heads * out_dim), bwd_map)],
        out_shape=[out, out],
        scratch_shapes=[state, state],
        compiler_params=pltpu.CompilerParams(dimension_semantics=("arbitrary",), vmem_limit_bytes=VMEM_LIMIT),
        name=name,
    )(*shared, *fwd, *shared, *bwd)


def _token_rows(tc, tl):
    t = jnp.concatenate([tc[0], tl[0]], axis=0)
    return t.reshape(t.shape[0], -1)


def _head_major(tc, tl):
    return jnp.transpose(jnp.concatenate([tc[0], tl[0]], axis=0), (1, 0, 2))


def _token_major(y, n_ctx):
    y = jnp.transpose(y, (1, 0, 2))[None]
    return y[:, :n_ctx], y[:, n_ctx:]


SEG_ROWS = 256
MM_TM = 1056
IN_TN = 896


def _modulate_kernel(x, scale, shift, o):
    o[...] = (x[...] * (1.0 + scale[0]) + shift[0]).astype(o.dtype)


def _modulate(x, scale, shift, n_ctx):
    n, d = x.shape
    assert n_ctx == SEG_ROWS and n % SEG_ROWS == 0
    seg = lambda i: (jnp.minimum(i, 1), 0, 0)
    return pl.pallas_call(
        _modulate_kernel,
        grid=(n // SEG_ROWS,),
        in_specs=[pl.BlockSpec((SEG_ROWS, d), lambda i: (i, 0)),
                  pl.BlockSpec((1, 1, d), seg), pl.BlockSpec((1, 1, d), seg)],
        out_specs=pl.BlockSpec((SEG_ROWS, d), lambda i: (i, 0)),
        out_shape=jax.ShapeDtypeStruct((n, d), jnp.bfloat16),
        compiler_params=pltpu.CompilerParams(dimension_semantics=("arbitrary",), vmem_limit_bytes=VMEM_LIMIT),
        name="modulate",
    )(x, scale[:, None, :], shift[:, None, :])


def _modulated_matmul_kernel(x, scale, shift, w, b, o, h, *, tm, n_ctx):
    @pl.when(pl.program_id(1) == 0)
    def _():
        rows = pl.program_id(0) * tm + lax.broadcasted_iota(jnp.int32, (tm, 1), 0)
        is_ctx = rows < n_ctx
        sc = jnp.where(is_ctx, scale[0:1, :], scale[1:2, :])
        sh = jnp.where(is_ctx, shift[0:1, :], shift[1:2, :])
        h[...] = (x[...] * (1.0 + sc) + sh).astype(h.dtype)

    o[...] = jnp.dot(h[...], w[0], preferred_element_type=jnp.float32) + b[0]


def _modulated_matmul_bias(x, scale, shift, w, b, layer, n_ctx, tm, tn):
    m, k = x.shape
    n = w.shape[2]
    assert m % tm == 0 and n % tn == 0
    return pl.pallas_call(
        functools.partial(_modulated_matmul_kernel, tm=tm, n_ctx=n_ctx),
        grid=(m // tm, n // tn),
        in_specs=[pl.BlockSpec((tm, k), lambda i, j: (i, 0)),
                  pl.BlockSpec((2, k), lambda i, j: (0, 0)),
                  pl.BlockSpec((2, k), lambda i, j: (0, 0)),
                  pl.BlockSpec((1, k, tn), lambda i, j: (layer, 0, j)),
                  pl.BlockSpec((1, 1, tn), lambda i, j: (layer, 0, j))],
        out_specs=pl.BlockSpec((tm, tn), lambda i, j: (i, j)),
        out_shape=jax.ShapeDtypeStruct((m, n), jnp.float32),
        scratch_shapes=[pltpu.VMEM((tm, k), jnp.bfloat16)],
        compiler_params=pltpu.CompilerParams(dimension_semantics=("arbitrary", "arbitrary"),
                                             vmem_limit_bytes=VMEM_LIMIT),
        name="modulated_matmul_bias",
    )(x, scale, shift, w, b)


def _matmul_bias_kernel(x, w, b, o):
    o[...] = jnp.dot(x[...], w[0], preferred_element_type=jnp.float32) + b[0]


def _matmul_bias(x, w, b, layer, tm, tn):
    m, k = x.shape
    n = w.shape[2]
    assert m % tm == 0 and n % tn == 0
    return pl.pallas_call(
        _matmul_bias_kernel,
        grid=(m // tm, n // tn),
        in_specs=[pl.BlockSpec((tm, k), lambda i, j: (i, 0)),
                  pl.BlockSpec((1, k, tn), lambda i, j: (layer, 0, j)),
                  pl.BlockSpec((1, 1, tn), lambda i, j: (layer, 0, j))],
        out_specs=pl.BlockSpec((tm, tn), lambda i, j: (i, j)),
        out_shape=jax.ShapeDtypeStruct((m, n), jnp.float32),
        compiler_params=pltpu.CompilerParams(dimension_semantics=("arbitrary", "arbitrary"),
                                             vmem_limit_bytes=VMEM_LIMIT),
        name="matmul_bias",
    )(x, w, b)


_GLA_MAIN = GLA_PROJ - 2 * GLA_LR
_GDN_MAIN = GDN_PROJ - 4 * GDN_HEADS
_LR_PAD = LANE - 2 * GLA_LR
_AB_PAD = LANE - 4 * GDN_HEADS
OFF_GLA = 0
OFF_RWKV = _GLA_MAIN + LANE
OFF_GDN = OFF_RWKV + RWKV_PROJ
OFF_GATE = OFF_GDN + _GDN_MAIN + LANE
IN_PAD = OFF_GATE + 3 * D_MODEL


def _pad_in_cols(w):
    o_rwkv, o_gdn, o_gate = GLA_PROJ, GLA_PROJ + RWKV_PROJ, GLA_PROJ + RWKV_PROJ + GDN_PROJ
    z = lambda n: jnp.zeros(w.shape[:-1] + (n,), w.dtype)
    return jnp.concatenate([w[..., :o_rwkv], z(_LR_PAD), w[..., o_rwkv:o_gdn], w[..., o_gdn:o_gate], z(_AB_PAD),
                            w[..., o_gate:]], axis=-1)


def _gla_mixer(pc, pl_, dec_w2, dec_b, norm_w):
    n_ctx = pc.shape[1]

    def prep(p):
        bsz, n = p.shape[:2]
        q, k, v, og, lr_f, lr_b = _split_cols(p, GLA_COLS)
        heads = lambda t, d: t.reshape(bsz, n, GLA_HEADS, d)
        g = [heads(jax.nn.log_sigmoid((_mm(lr, dec_w2[d]) + dec_b[d]).astype(jnp.float32)) / GLA_GATE_NORM, GLA_DK)
             for d, lr in enumerate((lr_f, lr_b))]
        return heads(q, GLA_DK) * GLA_DK ** -0.5, heads(k, GLA_DK), heads(v, GLA_DV), og, g

    qc, kc, vc, ogc, gc = prep(pc)
    ql, kl, vl, ogl, gl = prep(pl_)
    of, ob = _bidir_scan_tokens(_gla_chain, (_token_rows(qc, ql), _token_rows(kc, kl), _token_rows(vc, vl)),
                                (_token_rows(gc[0], gl[0]),), (_token_rows(gc[1], gl[1]),),
                                GLA_HEADS, GLA_DV, (GLA_DK, GLA_DV), n_ctx, "gla_scan")
    o = (of + ob).reshape(1, -1, GLA_HEADS, GLA_DV)
    oc, ol = o[:, :n_ctx], o[:, n_ctx:]

    def finish(o, og):
        return _rms_norm(o, norm_w).reshape(og.shape) * jax.nn.silu(og)

    return finish(oc, ogc), finish(ol, ogl)


def _rwkv_mixer(pc, pl_, mu, w2, w0, a2, a0, g2, k_k, k_a, r_k, ln_w, ln_b):
    n_ctx = pc.shape[1]

    def prep(p):
        bsz, n = p.shape[:2]
        heads = lambda t: t.reshape(bsz, n, RWKV_HEADS, RWKV_HEAD)
        p = p + (_token_shift(p) - p) * mu
        r, k, v, wl_f, wl_b, al_f, al_b, gl = _split_cols(p, RWKV_COLS)
        kk = _l2_normalize(heads(k * k_k))
        per_dir = []
        for d, (wl, al) in enumerate(((wl_f, al_f), (wl_b, al_b))):
            z = w0[d] + _mm(jnp.tanh(wl), w2[d])
            log_decay = -jnp.exp(-jax.nn.softplus(-z) - 0.5)
            a = jax.nn.sigmoid(a0[d] + _mm(al, a2[d]))
            per_dir.append((heads(log_decay), heads(k * (1 + (a - 1) * k_a)), kk * heads(a)))
        g = _mm(jax.nn.sigmoid(gl), g2)
        return heads(r), heads(k), heads(v), kk, per_dir, g

    rc, kc, vc, kkc, dc, gc = prep(pc)
    rl, kl, vl, kkl, dl, gl = prep(pl_)
    yf, yb = _bidir_scan_tokens(_rwkv_chain, (_token_rows(rc, rl), _token_rows(vc, vl), _token_rows(kkc, kkl)),
                                tuple(_token_rows(tc, tl) for tc, tl in zip(dc[0], dl[0])),
                                tuple(_token_rows(tc, tl) for tc, tl in zip(dc[1], dl[1])),
                                RWKV_HEADS, RWKV_HEAD, (RWKV_HEAD, RWKV_HEAD), n_ctx, "rwkv7_scan")
    y = (yf + yb).reshape(1, -1, RWKV_HEADS, RWKV_HEAD)
    yc, yl = y[:, :n_ctx], y[:, n_ctx:]

    def finish(y, r, k, v, g):
        yf = y.astype(jnp.float32)
        m = yf.mean(-1, keepdims=True)
        var = jnp.square(yf - m).mean(-1, keepdims=True)
        y = ((yf - m) * lax.rsqrt(var + RWKV_GN_EPS)).astype(y.dtype)
        y = y * ln_w.reshape(RWKV_HEADS, RWKV_HEAD) + ln_b.reshape(RWKV_HEADS, RWKV_HEAD)
        y = y + jnp.sum(r * k * r_k, axis=-1, keepdims=True) * v
        return y.reshape(g.shape) * g

    return finish(yc, rc, kc, vc, gc), finish(yl, rl, kl, vl, gl)


GDN_DEFAULT_PRECISION_WEIGHT = 0.75


def _to_chunks(t):
    b, n = t.shape[:2]
    t = t.reshape((b, n // CHUNK, CHUNK) + t.shape[2:])
    return jnp.swapaxes(t, 2, 3)


def _from_chunks(t):
    t = jnp.swapaxes(t, 2, 3)
    return t.reshape((t.shape[0], t.shape[1] * t.shape[2]) + t.shape[3:])


def _two_stage(scan_fn, ctx_args, lat_args, state0, reverse):
    if reverse:
        ctx_args = tuple(jnp.flip(t, axis=1) for t in ctx_args)
        lat_args = tuple(jnp.flip(t, axis=1) for t in lat_args)
    o_ctx, s_ctx = scan_fn(*ctx_args, state0)
    o_lat, _ = scan_fn(*lat_args, s_ctx)
    if reverse:
        o_ctx, o_lat = jnp.flip(o_ctx, axis=1), jnp.flip(o_lat, axis=1)
    return o_ctx, o_lat


def _gdn_chunked(q, k, v, log_a, beta, s0):
    dt = v.dtype
    f32 = jnp.float32
    q, k, v, log_a, beta = (_to_chunks(t.astype(f32)) for t in (q, k, v, log_a, beta))
    gam = jnp.cumsum(log_a, axis=-1)
    incl = jnp.tril(jnp.ones((CHUNK, CHUNK), bool))
    strict = jnp.tril(jnp.ones((CHUNK, CHUNK), bool), -1)
    decay = jnp.exp(jnp.where(incl, gam[..., :, None] - gam[..., None, :], -jnp.inf))
    a_mat = jnp.where(strict, beta[..., :, None] * jnp.einsum('bnhid,bnhjd->bnhij', k, k) * decay, 0.0)
    rhs = jnp.concatenate([v * beta[..., None], k * (beta * jnp.exp(gam))[..., None]], axis=-1)
    sol = lax.linalg.triangular_solve(a_mat, rhs, left_side=True, lower=True, unit_diagonal=True)
    dv = v.shape[-1]
    u, w = sol[..., :dv], sol[..., dv:]
    qk = jnp.einsum('bnhid,bnhjd->bnhij', q, k) * decay
    q_dec = q * jnp.exp(gam)[..., None]
    k_end = k * jnp.exp(gam[..., -1:] - gam)[..., None]
    dec = jnp.exp(gam[..., -1])

    def step(s, inp):
        u_n, w_n, qk_n, q_n, k_n, dec_n = inp
        delta = u_n - jnp.einsum('bhcd,bhdv->bhcv', w_n, s)
        o_n = jnp.einsum('bhcd,bhdv->bhcv', q_n, s) + jnp.einsum('bhij,bhjv->bhiv', qk_n, delta)
        s = dec_n[..., None, None] * s + jnp.einsum('bhcd,bhcv->bhdv', k_n, delta)
        return s, o_n

    xs = tuple(jnp.moveaxis(t, 1, 0) for t in (u, w, qk, q_dec, k_end, dec))
    s_fin, o = lax.scan(step, s0, xs)
    return _from_chunks(jnp.moveaxis(o, 0, 1)).astype(dt), s_fin


def _gdn_mixer(pc, pl_, conv_w, a_log, dt_bias, norm_w):
    n_ctx = pc.shape[1]

    def prep(p, on_grid):
        bsz, n = p.shape[:2]
        heads = lambda t: t.reshape(bsz, n, GDN_HEADS, GDN_HEAD)
        qkv, z, a_f, a_b, b_f, b_b = _split_cols(p, GDN_COLS)
        qkv = jax.nn.silu(_grid_conv(qkv, conv_w) if on_grid else _seq_conv(qkv, conv_w[CONV_K // 2]))
        q, k, v = jnp.split(qkv, 3, axis=-1)
        per_dir = [(-jnp.exp(a_log[d]) * jax.nn.softplus(a + dt_bias[d]), jax.nn.sigmoid(b))
                   for d, (a, b) in enumerate(((a_f, b_f), (a_b, b_b)))]
        return _l2_normalize(heads(q)) * GDN_HEAD ** -0.5, _l2_normalize(heads(k)), heads(v), z, per_dir

    qc, kc, vc, zc, dc = prep(pc, False)
    ql, kl, vl, zl, dl = prep(pl_, True)

    def lanes(tc, tl):
        t = jnp.transpose(jnp.concatenate([tc[0], tl[0]], axis=0), (1, 0))
        return jnp.broadcast_to(t[:, :, None], t.shape + (GDN_HEAD,))

    of, ob = _bidir_scan(_gdn_chain, (_head_major(qc, ql), _head_major(kc, kl), _head_major(vc, vl)),
                         tuple(lanes(tc, tl) for tc, tl in zip(dc[0], dl[0])),
                         tuple(lanes(tc, tl) for tc, tl in zip(dc[1], dl[1])),
                         GDN_HEAD, (GDN_HEAD, GDN_HEAD), GDN_HEADS, n_ctx, "gdn_scan")
    pc_o, pl_o = _token_major(of + ob, n_ctx)
    s0 = jnp.zeros((pc.shape[0], GDN_HEADS, GDN_HEAD, GDN_HEAD), jnp.float32)
    cf, lf = _two_stage(_gdn_chunked, (qc, kc, vc) + dc[0], (ql, kl, vl) + dl[0], s0, False)
    cb, lb = _two_stage(_gdn_chunked, (qc, kc, vc) + dc[1], (ql, kl, vl) + dl[1], s0, True)
    mix = lambda a, b: GDN_DEFAULT_PRECISION_WEIGHT * a + (1.0 - GDN_DEFAULT_PRECISION_WEIGHT) * b
    oc, ol = mix(cf + cb, pc_o), mix(lf + lb, pl_o)

    def finish(o, z):
        return _rms_norm(o, norm_w).reshape(z.shape) * jax.nn.silu(z)

    return finish(oc, zc), finish(ol, zl)


def _merge(gate_pre, outs, w_branch, w_out):
    gates = jax.nn.sigmoid(gate_pre)
    y = 0
    for m, o in enumerate(outs):
        y = y + gates[..., m * D_MODEL:(m + 1) * D_MODEL] * _mm(o, w_branch[m])
    return _mm(y, w_out)


def _expert_kernel(blk_e, n_used, x, w1, w3, w2, o):
    i = pl.program_id(0)

    @pl.when(i < n_used[0])
    def _():
        xe = x[...]
        h1 = jnp.dot(xe, w1[0, 0], preferred_element_type=jnp.float32)
        h3 = jnp.dot(xe, w3[0, 0], preferred_element_type=jnp.float32)
        o[...] = jnp.dot(_bf(jax.nn.silu(h1) * h3), w2[0, 0], preferred_element_type=jnp.float32)

    @pl.when(i >= n_used[0])
    def _():
        o[...] = jnp.zeros_like(o)


def _expert_blocks(blk_e, n_used, xb, w1, w3, w2, layer):
    n_slots, d = xb.shape
    de = w1.shape[3]
    n_blk = n_slots // DISPATCH_BLOCK
    grid_spec = pltpu.PrefetchScalarGridSpec(
        num_scalar_prefetch=2,
        grid=(n_blk,),
        in_specs=[pl.BlockSpec((DISPATCH_BLOCK, d), lambda i, be, nu: (i, 0)),
                  pl.BlockSpec((1, 1, d, de), lambda i, be, nu: (layer, be[i], 0, 0)),
                  pl.BlockSpec((1, 1, d, de), lambda i, be, nu: (layer, be[i], 0, 0)),
                  pl.BlockSpec((1, 1, de, d), lambda i, be, nu: (layer, be[i], 0, 0))],
        out_specs=pl.BlockSpec((DISPATCH_BLOCK, d), lambda i, be, nu: (i, 0)),
    )
    return pl.pallas_call(
        _expert_kernel,
        grid_spec=grid_spec,
        out_shape=jax.ShapeDtypeStruct((n_slots, d), jnp.float32),
        compiler_params=pltpu.CompilerParams(dimension_semantics=("arbitrary",), vmem_limit_bytes=VMEM_LIMIT),
        name="moe_experts",
    )(blk_e, n_used, xb, w1, w3, w2)


def _moe_ffn(h, router_w, router_bias, w1, w3, w2, layer):
    n_tok, d = h.shape
    scores = jax.nn.sigmoid(_mm(h, router_w))
    sel = scores + router_bias.astype(jnp.float32)
    grp_top = lax.top_k(sel.reshape(n_tok, N_GROUPS, N_EXPERTS // N_GROUPS), 2)[0]
    best_group = jnp.argmax(grp_top.sum(-1), axis=-1)
    in_group = (jnp.arange(N_EXPERTS) // (N_EXPERTS // N_GROUPS))[None, :] == best_group[:, None]
    _, top_e = lax.top_k(jnp.where(in_group, sel, -jnp.inf), TOP_K)
    top_s = jnp.take_along_axis(scores, top_e, axis=-1)
    top_w = (top_s / top_s.sum(-1, keepdims=True)).astype(h.dtype)
    n_asg = n_tok * TOP_K
    flat_e = top_e.reshape(-1)
    order = jnp.argsort(flat_e)
    sorted_e = flat_e[order]
    counts = jnp.bincount(flat_e, length=N_EXPERTS)
    padded = (counts + DISPATCH_BLOCK - 1) // DISPATCH_BLOCK * DISPATCH_BLOCK
    pad_end = jnp.cumsum(padded)
    first = jnp.cumsum(counts) - counts
    dest = (pad_end - padded)[sorted_e] + jnp.arange(n_asg) - first[sorted_e]
    n_blk = -(-n_asg // DISPATCH_BLOCK) + N_EXPERTS
    tok = jnp.repeat(jnp.arange(n_tok, dtype=jnp.int32), TOP_K)
    slot_tok = jnp.full((n_blk * DISPATCH_BLOCK,), n_tok, jnp.int32).at[dest].set(tok[order])
    blk_e = jnp.minimum(jnp.searchsorted(pad_end, jnp.arange(n_blk) * DISPATCH_BLOCK, side='right'), N_EXPERTS - 1)
    h_pad = jnp.concatenate([_bf(h), jnp.zeros((1, d), jnp.bfloat16)], axis=0)
    n_used = (pad_end[-1] // DISPATCH_BLOCK).astype(jnp.int32).reshape(1)
    yb = _expert_blocks(blk_e.astype(jnp.int32), n_used, h_pad[slot_tok], w1, w3, w2, layer)
    slot_of = jnp.zeros((n_asg,), jnp.int32).at[order].set(dest.astype(jnp.int32)).reshape(n_tok, TOP_K)
    y = 0
    for j in range(TOP_K):
        y = y + yb[slot_of[:, j]] * top_w[:, j:j + 1]
    return y


def kernel(x, c, ctx, c_ctx, w_ada, b_ada, w_in, b_in,
           gla_dec_w, gla_dec_b, gla_norm_w,
           rwkv_mu, rwkv_w2, rwkv_w0, rwkv_a2, rwkv_a0, rwkv_g2, rwkv_kk, rwkv_ka, rwkv_rk,
           rwkv_ln_w, rwkv_ln_b,
           gdn_conv_w, gdn_a_log, gdn_dt_bias, gdn_norm_w,
           w_branch, w_out, ln1_g, ln1_b,
           router_w, router_bias, moe_w1, moe_w3, moe_w2, ln2_g, ln2_b):
    alpha = (2.0 * DEPTH) ** 0.25
    bsz, n_lat, d = x.shape
    n_ctx = ctx.shape[1]
    silu_c = jax.nn.silu(c)[:, None, :]
    silu_cc = jax.nn.silu(c_ctx)
    xl, xc = x, ctx
    w_in_p, b_in_p = _bf(_pad_in_cols(w_in)), _pad_in_cols(b_in)[:, None, :]
    w1_b, w3_b, w2_b = _bf(moe_w1), _bf(moe_w3), _bf(moe_w2)
    for l in range(DEPTH):
        last = l == DEPTH - 1
        mod_l = jnp.split(_mm(silu_c, w_ada[l]) + b_ada[l], 6, axis=-1)
        mod_c = jnp.split(_mm(silu_cc, w_ada[l]) + b_ada[l], 6, axis=-1)
        p = _modulated_matmul_bias(jnp.concatenate([xc[0], xl[0]], axis=0),
                                   jnp.stack([mod_c[1], mod_l[1][0, 0]]), jnp.stack([mod_c[0], mod_l[0][0, 0]]),
                                   w_in_p, b_in_p, l, n_ctx, MM_TM, IN_TN)
        slab = lambda off, width: (p[None, :n_ctx, off:off + width], p[None, n_ctx:, off:off + width])
        gla_c, gla_l = slab(OFF_GLA, GLA_PROJ)
        rwkv_c, rwkv_l = slab(OFF_RWKV, RWKV_PROJ)
        gdn_c, gdn_l = slab(OFF_GDN, GDN_PROJ)
        gate_c, gate_l = slab(OFF_GATE, 3 * D_MODEL)
        oa_c, oa_l = _gla_mixer(gla_c, gla_l, gla_dec_w[l], gla_dec_b[l], gla_norm_w[l])
        ob_c, ob_l = _rwkv_mixer(rwkv_c, rwkv_l, rwkv_mu[l], rwkv_w2[l], rwkv_w0[l], rwkv_a2[l], rwkv_a0[l],
                                 rwkv_g2[l], rwkv_kk[l], rwkv_ka[l], rwkv_rk[l], rwkv_ln_w[l], rwkv_ln_b[l])
        oc_c, oc_l = _gdn_mixer(gdn_c, gdn_l, gdn_conv_w[l], gdn_a_log[l], gdn_dt_bias[l], gdn_norm_w[l])
        mix_l = _merge(gate_l, (oa_l, ob_l, oc_l), w_branch[l], w_out[l])
        xl = _layer_norm(alpha * xl + mod_l[2] * mix_l, ln1_g[l], ln1_b[l])
        h2l = xl * (1 + mod_l[4]) + mod_l[3]
        if last:
            yl = _moe_ffn(h2l.reshape(-1, d), router_w, router_bias, w1_b, w3_b, w2_b, l).reshape(xl.shape)
        else:
            mix_c = _merge(gate_c, (oa_c, ob_c, oc_c), w_branch[l], w_out[l])
            xc = _layer_norm(alpha * xc + mod_c[2] * mix_c, ln1_g[l], ln1_b[l])
            h2c = xc * (1 + mod_c[4]) + mod_c[3]
            y = _moe_ffn(jnp.concatenate([h2c, h2l], axis=1).reshape(-1, d), router_w, router_bias,
                         w1_b, w3_b, w2_b, l).reshape(bsz, n_ctx + n_lat, d)
            xc = _layer_norm(alpha * xc + mod_c[5] * y[:, :n_ctx], ln2_g[l], ln2_b[l])
            yl = y[:, n_ctx:]
        xl = _layer_norm(alpha * xl + mod_l[5] * yl, ln2_g[l], ln2_b[l])
    return xl
```
